```python
import math
import jax, jax.numpy as jnp
from jax import lax
import numpy as np

D_MODEL = 4096
BATCH = 2
SEQ = 8192
DEPTH = 2

CHUNK = 64
Q_BLOCK = 128
HEAD_DIM = 128
HALF_DIM = HEAD_DIM // 2
N_HEADS = D_MODEL // HEAD_DIM
H_DIFF = (N_HEADS * 5) // 16
H_FOX = (N_HEADS * 5) // 16
H_SB = N_HEADS - H_DIFF - H_FOX
W_SB = H_SB * HEAD_DIM
W_DIFF = H_DIFF * HEAD_DIM
W_FOX = H_FOX * HEAD_DIM
N_IN = 3 * (W_SB + W_DIFF + W_FOX) + H_FOX
N_BUCKETS = 32
MAX_DISTANCE = 128
N_GROUPS = 4
EXPERTS_PER_GROUP = 8
N_EXPERTS = N_GROUPS * EXPERTS_PER_GROUP
TOP_K = 2
D_EXPERT = 512
ROW_BLOCK = 128
FORGET_BIAS_INIT = 3.0
EPS = 1e-6

kernel_name = "chunk_causal_hybrid_heads_hmoe"


def rms_norm(x, gain):
    xf = x.astype(jnp.float32)
    y = xf * lax.rsqrt(jnp.mean(xf * xf, axis=-1, keepdims=True) + EPS)
    return (y * gain.astype(jnp.float32)).astype(x.dtype)


def to_heads(t, n_heads):
    b, s, _ = t.shape
    return t.reshape(b, s, n_heads, -1).transpose(0, 2, 1, 3)


def from_heads(t):
    b, h, s, d = t.shape
    return t.transpose(0, 2, 1, 3).reshape(b, s, h * d)


def relative_bucket(rel):
    half = N_BUCKETS // 2
    max_exact = half // 2
    n = jnp.abs(rel)
    large = max_exact + (jnp.log(jnp.maximum(n, 1).astype(jnp.float32) / max_exact)
                         / math.log(MAX_DISTANCE / max_exact) * (half - max_exact)).astype(jnp.int32)
    large = jnp.minimum(large, half - 1)
    return jnp.where(rel > 0, half, 0) + jnp.where(n < max_exact, n, large)


def stick_breaking_block(q, k, v, strict):
    z = jnp.einsum("bhqd,bhkd->bhqk", q, k, preferred_element_type=jnp.float32) / math.sqrt(q.shape[-1])
    log_keep = jnp.where(strict, jax.nn.log_sigmoid(-z), 0.0)
    after = lax.cumsum(log_keep, axis=3, reverse=True) - log_keep
    w = jnp.where(strict, jnp.exp(jax.nn.log_sigmoid(z) + after), 0.0)
    return jnp.einsum("bhqk,bhkd->bhqd", w.astype(v.dtype), v)


def differential_block(q1, q2, k1, k2, v, bias, mask, lam):
    scale = 1.0 / math.sqrt(q1.shape[-1])

    def attn_map(q, k):
        s = jnp.einsum("bhqd,bhkd->bhqk", q, k, preferred_element_type=jnp.float32) * scale + bias
        return jax.nn.softmax(jnp.where(mask, s, -jnp.inf), axis=-1)

    w = attn_map(q1, k1) - lam * attn_map(q2, k2)
    return jnp.einsum("bhqk,bhkd->bhqd", w.astype(v.dtype), v)


def forgetting_block(q, k, v, cum_q, cum_k, mask):
    s = jnp.einsum("bhqd,bhkd->bhqk", q, k, preferred_element_type=jnp.float32) / math.sqrt(q.shape[-1])
    s = s + (cum_q[..., :, None] - cum_k[..., None, :])
    p = jax.nn.softmax(jnp.where(mask, s, -jnp.inf), axis=-1)
    return jnp.einsum("bhqk,bhkd->bhqd", p.astype(v.dtype), v)


def token_mixers(u, w_in, b_f, q_gain_diff, k_gain_diff, lam, subln_gain, q_gain_fox, k_gain_fox,
                 out_gain_sb, out_gain_fox, rel_bias, w_out, layer):
    b, s_len, _ = u.shape
    proj = jnp.einsum("bsd,dn->bsn", u, w_in)
    cuts = [int(c) for c in np.cumsum([W_SB] * 3 + [W_DIFF] * 3 + [W_FOX] * 3)]
    qa, ka, va, qb, kb, vb, qc, kc, vc, fc = jnp.split(proj, cuts, axis=-1)

    qa, ka, va = to_heads(qa, H_SB), to_heads(ka, H_SB), to_heads(va, H_SB)

    qb = rms_norm(qb.reshape(b, s_len, H_DIFF, 2, HALF_DIM), q_gain_diff).transpose(0, 2, 3, 1, 4)
    kb = rms_norm(kb.reshape(b, s_len, H_DIFF, 2, HALF_DIM), k_gain_diff).transpose(0, 2, 3, 1, 4)
    q1, q2, k1, k2 = qb[:, :, 0], qb[:, :, 1], kb[:, :, 0], kb[:, :, 1]
    vb = to_heads(vb, H_DIFF)
    lam_init = 0.8 - 0.6 * math.exp(-0.3 * layer)
    lamf = lam.astype(jnp.float32)
    lam_val = jnp.exp(jnp.sum(lamf[0] * lamf[1])) - jnp.exp(jnp.sum(lamf[2] * lamf[3])) + lam_init

    qc = rms_norm(to_heads(qc, H_FOX), q_gain_fox)
    kc = rms_norm(to_heads(kc, H_FOX), k_gain_fox)
    vc = to_heads(vc, H_FOX)
    log_f = jax.nn.log_sigmoid((fc + b_f).astype(jnp.float32))
    cum = jnp.cumsum(log_f, axis=1).transpose(0, 2, 1)

    pos = np.arange(s_len)
    o_sb, o_diff, o_fox = [], [], []
    for blk in range(s_len // Q_BLOCK):
        q0, q_end = blk * Q_BLOCK, (blk + 1) * Q_BLOCK
        t = pos[q0:q_end][:, None]
        s = pos[:q_end][None, :]
        strict = s < t
        causal = s <= t
        chunk_causal = (s // CHUNK) <= (t // CHUNK)
        bias = rel_bias[relative_bucket(s - t)].transpose(2, 0, 1)
        o_sb.append(stick_breaking_block(qa[:, :, q0:q_end], ka[:, :, :q_end], va[:, :, :q_end], strict))
        o_diff.append(differential_block(q1[:, :, q0:q_end], q2[:, :, q0:q_end], k1[:, :, :q_end],
                                         k2[:, :, :q_end], vb[:, :, :q_end], bias, chunk_causal, lam_val))
        o_fox.append(forgetting_block(qc[:, :, q0:q_end], kc[:, :, :q_end], vc[:, :, :q_end],
                                      cum[:, :, q0:q_end], cum[:, :, :q_end], causal))

    y_sb = rms_norm(jnp.concatenate(o_sb, axis=2), out_gain_sb)
    y_diff = rms_norm(jnp.concatenate(o_diff, axis=2), subln_gain) * (1.0 - lam_init)
    y_fox = rms_norm(jnp.concatenate(o_fox, axis=2), out_gain_fox)
    merged = jnp.concatenate([from_heads(y_sb), from_heads(y_diff), from_heads(y_fox)], axis=-1)
    return jnp.einsum("bsd,de->bse", merged, w_out)


def hierarchical_moe(u, w_coarse, b_coarse, w_fine, b_fine, w_gate, w_up, w_down):
    b, s_len, d = u.shape
    t = b * s_len
    xt = u.reshape(t, d)
    p_coarse = jax.nn.softmax((xt @ w_coarse + b_coarse).astype(jnp.float32), axis=-1)
    g_idx = jnp.argmax(p_coarse, axis=-1)
    g_prob = jnp.take_along_axis(p_coarse, g_idx[:, None], axis=-1)
    fine = (xt @ w_fine + b_fine).astype(jnp.float32).reshape(t, N_GROUPS, EXPERTS_PER_GROUP)
    p_fine = jax.nn.softmax(fine[jnp.arange(t), g_idx], axis=-1)
    e_prob, e_local = lax.top_k(p_fine, TOP_K)
    gate = g_prob * e_prob / jnp.sum(e_prob, axis=-1, keepdims=True)
    expert = g_idx[:, None] * EXPERTS_PER_GROUP + e_local

    n_assign = t * TOP_K
    flat_e = expert.reshape(-1).astype(jnp.int32)
    flat_tok = jnp.repeat(jnp.arange(t, dtype=jnp.int32), TOP_K)
    order = jnp.argsort(flat_e)
    sorted_e = flat_e[order]
    counts = jnp.bincount(flat_e, length=N_EXPERTS)
    starts = jnp.cumsum(counts) - counts
    padded = ((counts + ROW_BLOCK - 1) // ROW_BLOCK) * ROW_BLOCK
    pends = jnp.cumsum(padded)
    pstarts = pends - padded
    dest = pstarts[sorted_e] + (jnp.arange(n_assign) - starts[sorted_e])
    n_blocks = -(-(n_assign + N_EXPERTS * (ROW_BLOCK - 1)) // ROW_BLOCK)
    n_rows = n_blocks * ROW_BLOCK
    row_tok = jnp.full((n_rows,), t, jnp.int32).at[dest].set(flat_tok[order])
    row_gate = jnp.zeros((n_rows,), jnp.float32).at[dest].set(gate.reshape(-1)[order])
    block_e = jnp.minimum(jnp.searchsorted(pends, jnp.arange(n_blocks) * ROW_BLOCK, side="right"),
                          N_EXPERTS - 1).astype(jnp.int32)

    def run_block(args):
        rows, e = args
        xb = jnp.take(xt, rows, axis=0, mode="fill", fill_value=0)
        hid = jax.nn.silu(xb @ w_gate[e]) * (xb @ w_up[e])
        return hid @ w_down[e]

    out = lax.map(run_block, (row_tok.reshape(n_blocks, ROW_BLOCK), block_e)).reshape(n_rows, d)
    out = out * row_gate[:, None].astype(out.dtype)
    y = jax.ops.segment_sum(out, row_tok, num_segments=t)
    return y.reshape(b, s_len, d).astype(u.dtype)


def setup_inputs(seed: int = 0) -> dict:
    key = jax.random.key(seed)
    ks = jax.random.split(key, 24)
    f32 = jnp.float32
    nrm = jax.random.normal
    d_scale = D_MODEL ** -0.5

    def gain(k, shape):
        return 1.0 + 0.02 * nrm(k, shape, f32)

    return {
        "x": nrm(ks[0], (BATCH, SEQ, D_MODEL), f32),
        "norm_attn": gain(ks[1], (DEPTH, D_MODEL)),
        "w_in": nrm(ks[2], (DEPTH, D_MODEL, N_IN), f32) * d_scale,
        "b_f": FORGET_BIAS_INIT + 0.1 * nrm(ks[3], (DEPTH, H_FOX), f32),
        "q_gain_diff": gain(ks[4], (DEPTH, 2, HALF_DIM)),
        "k_gain_diff": gain(ks[5], (DEPTH, 2, HALF_DIM)),
        "lam": 0.1 * nrm(ks[6], (DEPTH, 4, HALF_DIM), f32),
        "subln_gain": gain(ks[7], (DEPTH, HEAD_DIM)),
        "q_gain_fox": gain(ks[8], (DEPTH, HEAD_DIM)),
        "k_gain_fox": gain(ks[9], (DEPTH, HEAD_DIM)),
        "out_gain_sb": gain(ks[10], (DEPTH, HEAD_DIM)),
        "out_gain_fox": gain(ks[11], (DEPTH, HEAD_DIM)),
        "rel_bias": 0.5 * nrm(ks[12], (N_BUCKETS, H_DIFF), f32),
        "w_out": nrm(ks[13], (DEPTH, D_MODEL, D_MODEL), f32) * d_scale,
        "norm_ffn": gain(ks[14], (DEPTH, D_MODEL)),
        "w_coarse": nrm(ks[15], (DEPTH, D_MODEL, N_GROUPS), f32) * d_scale,
        "b_coarse": 0.01 * nrm(ks[16], (DEPTH, N_GROUPS), f32),
        "w_fine": nrm(ks[17], (DEPTH, D_MODEL, N_EXPERTS), f32) * d_scale,
        "b_fine": 0.01 * nrm(ks[18], (DEPTH, N_EXPERTS), f32),
        "w_gate": nrm(ks[19], (DEPTH, N_EXPERTS, D_MODEL, D_EXPERT), f32) * d_scale,
        "w_up": nrm(ks[20], (DEPTH, N_EXPERTS, D_MODEL, D_EXPERT), f32) * d_scale,
        "w_down": nrm(ks[21], (DEPTH, N_EXPERTS, D_EXPERT, D_MODEL), f32) * D_EXPERT ** -0.5,
    }


def reference(x, norm_attn, w_in, b_f, q_gain_diff, k_gain_diff, lam, subln_gain, q_gain_fox, k_gain_fox,
              out_gain_sb, out_gain_fox, rel_bias, w_out, norm_ffn, w_coarse, b_coarse, w_fine, b_fine,
              w_gate, w_up, w_down):
    h = x
    for layer in range(DEPTH):
        u = rms_norm(h, norm_attn[layer])
        h = h + token_mixers(u, w_in[layer], b_f[layer], q_gain_diff[layer], k_gain_diff[layer], lam[layer],
                             subln_gain[layer], q_gain_fox[layer], k_gain_fox[layer], out_gain_sb[layer],
                             out_gain_fox[layer], rel_bias, w_out[layer], layer)
        u = rms_norm(h, norm_ffn[layer])
        h = h + hierarchical_moe(u, w_coarse[layer], b_coarse[layer], w_fine[layer], b_fine[layer],
                                 w_gate[layer], w_up[layer], w_down[layer])
    return h
```

```python
import functools
import math

import jax
import jax.numpy as jnp
from jax import lax
from jax.experimental import pallas as pl
from jax.experimental.pallas import tpu as pltpu

D_MODEL = 4096
BATCH = 2
SEQ = 8192
DEPTH = 2
CHUNK = 64
HEAD_DIM = 128
HALF_DIM = HEAD_DIM // 2
N_HEADS = D_MODEL // HEAD_DIM
H_DIFF = (N_HEADS * 5) // 16
H_FOX = (N_HEADS * 5) // 16
H_SB = N_HEADS - H_DIFF - H_FOX
N_BUCKETS = 32
MAX_DISTANCE = 128
N_GROUPS = 4
EXPERTS_PER_GROUP = 8
N_EXPERTS = N_GROUPS * EXPERTS_PER_GROUP
TOP_K = 2
D_EXPERT = 512
EPS = 1e-6

F32 = jnp.float32
BF16 = jnp.bfloat16
NEG = -1e30
SUBLANES_BF16 = 16
LANES = 128
GATE_ROWS = 16
ROUTER_FINE_ROW0 = 8
ROUTER_ROWS = 48

NORM_TM = 512
PROJ_TM = 1024
PROJ_TN = 1024
ATT_T = 256
KPREP_ROWS = 512
OUT_TM = 1024
OUT_TN = 512
MOE_TM = 256
COMBINE_TC = 256

_NT = (((1,), (1,)), ((), ()))


def _params(semantics, vmem_mib):
    return pltpu.CompilerParams(dimension_semantics=semantics, vmem_limit_bytes=vmem_mib << 20)


def _tile(n, pref, unit=LANES):
    t = min(pref, n) // unit * unit
    while n % t:
        t -= unit
    return t


def _rms(x, gain):
    return x * lax.rsqrt(jnp.mean(x * x, axis=-1, keepdims=True) + EPS) * gain


def _split_bf16(a):
    hi = a.astype(BF16)
    lo = (a - hi.astype(F32)).astype(BF16)
    return hi, lo


def _small_proj_t(w_hi, w_lo, y):
    y_hi, y_lo = _split_bf16(y)
    p = lax.dot_general(w_hi, y_hi, _NT, preferred_element_type=F32)
    p += lax.dot_general(w_lo, y_hi, _NT, preferred_element_type=F32)
    p += lax.dot_general(w_hi, y_lo, _NT, preferred_element_type=F32)
    return p, y_hi


def _attn_norm_kernel(x_ref, g_ref, wh_ref, wl_ref, u_ref, p_ref):
    y = _rms(x_ref[...], g_ref[...])
    p, y_hi = _small_proj_t(wh_ref[...], wl_ref[...], y)
    u_ref[...] = y_hi
    p_ref[...] = p


def _attn_norm(h, gain, w_hi, w_lo):
    t, d = h.shape
    tm = _tile(t, NORM_TM)
    r = w_hi.shape[0]
    return pl.pallas_call(
        _attn_norm_kernel,
        grid=(t // tm,),
        in_specs=[pl.BlockSpec((tm, d), lambda i: (i, 0)),
                  pl.BlockSpec((1, d), lambda i: (0, 0)),
                  pl.BlockSpec((r, d), lambda i: (0, 0)),
                  pl.BlockSpec((r, d), lambda i: (0, 0))],
        out_specs=[pl.BlockSpec((tm, d), lambda i: (i, 0)),
                   pl.BlockSpec((r, tm), lambda i: (0, i))],
        out_shape=[jax.ShapeDtypeStruct((t, d), BF16), jax.ShapeDtypeStruct((r, t), F32)],
        compiler_params=_params(("arbitrary",), 40),
        name="attn_norm",
    )(h, gain.reshape(1, d), w_hi, w_lo)


def _in_proj_kernel(u_ref, w_ref, o_ref):
    acc = jnp.dot(u_ref[...], w_ref[...], preferred_element_type=F32)
    for s in range(o_ref.shape[0]):
        o_ref[s] = acc[:, s * HEAD_DIM:(s + 1) * HEAD_DIM].astype(BF16)


def _in_proj(u, w):
    t, d = u.shape
    n = w.shape[1]
    tm, tn = _tile(t, PROJ_TM), _tile(n, PROJ_TN)
    return pl.pallas_call(
        _in_proj_kernel,
        grid=(t // tm, n // tn),
        in_specs=[pl.BlockSpec((tm, d), lambda i, j: (i, 0)),
                  pl.BlockSpec((d, tn), lambda i, j: (0, j))],
        out_specs=pl.BlockSpec((tn // HEAD_DIM, tm, HEAD_DIM), lambda i, j: (j, i, 0)),
        out_shape=jax.ShapeDtypeStruct((n // HEAD_DIM, t, HEAD_DIM), BF16),
        compiler_params=_params(("arbitrary", "arbitrary"), 52),
        name="in_proj",
    )(u, w)


def _gate_cumsum_kernel(fc_ref, b_ref, tri_ref, o_ref):
    x = fc_ref[...] + b_ref[...]
    log_f = jnp.minimum(x, 0.0) - jnp.log(1.0 + jnp.exp(-jnp.abs(x)))
    tri = tri_ref[...]
    carry = jnp.zeros((x.shape[0], 1), F32)
    for c in range(x.shape[1] // LANES):
        seg = log_f[:, c * LANES:(c + 1) * LANES]
        a = seg.astype(BF16)
        rest = seg - a.astype(F32)
        b, c3 = _split_bf16(rest)
        cs = (jnp.dot(a, tri, preferred_element_type=F32) + jnp.dot(b, tri, preferred_element_type=F32)
              + jnp.dot(c3, tri, preferred_element_type=F32)) + carry
        o_ref[:, c * LANES:(c + 1) * LANES] = cs
        carry = cs[:, LANES - 1:LANES]


def _gate_cumsum(fc_t, b_f, batch):
    r, t = fc_t.shape
    s = t // batch
    idx = jnp.arange(LANES)
    tri = (idx[:, None] <= idx[None, :]).astype(BF16)
    return pl.pallas_call(
        _gate_cumsum_kernel,
        grid=(batch,),
        in_specs=[pl.BlockSpec((r, s), lambda b: (0, b)),
                  pl.BlockSpec((r, 1), lambda b: (0, 0)),
                  pl.BlockSpec((LANES, LANES), lambda b: (0, 0))],
        out_specs=pl.BlockSpec((r, s), lambda b: (0, b)),
        out_shape=jax.ShapeDtypeStruct((r, t), F32),
        compiler_params=_params(("arbitrary",), 32),
        name="gate_cumsum",
    )(fc_t, b_f, tri)


def _kv_rows(j, tk):
    return pl.ds(pl.multiple_of(j * tk, tk), tk)


def _softmax_step(e, v, m_ref, l_ref, acc_ref):
    m_old = m_ref[...]
    m_new = jnp.maximum(m_old, jnp.max(e, axis=-1, keepdims=True))
    alpha = jnp.exp(m_old - m_new)
    p = jnp.exp(e - m_new)
    l_ref[...] = alpha * l_ref[...] + jnp.sum(p, axis=-1, keepdims=True)
    acc_ref[...] = alpha * acc_ref[...] + jnp.dot(p.astype(BF16), v, preferred_element_type=F32)
    m_ref[...] = m_new


def _head_specs(t_blocks_per_batch, tq, seq, q_slot, k_slot, v_slot):
    nq = t_blocks_per_batch
    return [pl.BlockSpec((1, tq, HEAD_DIM), lambda b, h, i: (q_slot + h, b * nq + i, 0)),
            pl.BlockSpec((1, seq, HEAD_DIM), lambda b, h, i: (k_slot + h, b, 0)),
            pl.BlockSpec((1, seq, HEAD_DIM), lambda b, h, i: (v_slot + h, b, 0))]


def _sb_kernel(q_ref, k_ref, v_ref, tri_ref, g_ref, o_ref, acc_ref, run_ref):
    i = pl.program_id(2)
    tq = q_ref.shape[1]
    q = (q_ref[0].astype(F32) * (1.0 / math.sqrt(HEAD_DIM))).astype(BF16)
    row = lax.broadcasted_iota(jnp.int32, (tq, tq), 0)
    col = lax.broadcasted_iota(jnp.int32, (tq, tq), 1)
    strict = col < row
    acc_ref[...] = jnp.zeros_like(acc_ref)
    run_ref[...] = jnp.zeros_like(run_ref)

    def block(j, masked):
        k = k_ref[0, _kv_rows(j, tq), :]
        v = v_ref[0, _kv_rows(j, tq), :]
        z = lax.dot_general(q, k, _NT, preferred_element_type=F32)
        sp = jnp.maximum(z, 0.0) + jnp.log(1.0 + jnp.exp(-jnp.abs(z)))
        if masked:
            sp = jnp.where(strict, sp, 0.0)
        suffix = jnp.dot(sp.astype(BF16), tri_ref[...], preferred_element_type=F32)
        w = jnp.exp(z - suffix - run_ref[...])
        if masked:
            w = jnp.where(strict, w, 0.0)
        acc_ref[...] += jnp.dot(w.astype(BF16), v, preferred_element_type=F32)
        run_ref[...] += suffix[:, 0:1]

    block(i, True)

    def body(step, carry):
        block(i - 1 - step, False)
        return carry

    lax.fori_loop(0, i, body, 0)
    o_ref[...] = _rms(acc_ref[...], g_ref[...]).astype(BF16)


def _sb_attention(proj, gain, batch, seq):
    t = proj.shape[1]
    tq = _tile(seq, ATT_T)
    nq = seq // tq
    idx = jnp.arange(tq)
    tri = (idx[:, None] >= idx[None, :]).astype(BF16)
    return pl.pallas_call(
        _sb_kernel,
        grid=(batch, H_SB, nq),
        in_specs=_head_specs(nq, tq, seq, 0, H_SB, 2 * H_SB) + [
            pl.BlockSpec((tq, tq), lambda b, h, i: (0, 0)),
            pl.BlockSpec((1, HEAD_DIM), lambda b, h, i: (0, 0))],
        out_specs=pl.BlockSpec((tq, HEAD_DIM), lambda b, h, i: (b * nq + i, h)),
        out_shape=jax.ShapeDtypeStruct((t, H_SB * HEAD_DIM), BF16),
        scratch_shapes=[pltpu.VMEM((tq, HEAD_DIM), F32), pltpu.VMEM((tq, 1), F32)],
        compiler_params=_params(("arbitrary", "arbitrary", "arbitrary"), 40),
        name="sb_attention",
    )(proj, proj, proj, tri, gain.reshape(1, HEAD_DIM))


def _sub_rms(x, gain):
    lo = lax.broadcasted_iota(jnp.int32, x.shape, 1) < HALF_DIM
    sq = x * x
    ms_lo = jnp.sum(jnp.where(lo, sq, 0.0), axis=-1, keepdims=True) * (1.0 / HALF_DIM)
    ms_hi = jnp.sum(jnp.where(lo, 0.0, sq), axis=-1, keepdims=True) * (1.0 / HALF_DIM)
    inv = jnp.where(lo, lax.rsqrt(ms_lo + EPS), lax.rsqrt(ms_hi + EPS))
    return x * inv * gain


def _diff_kernel(lam_init, lam_ref, q_ref, k_ref, v_ref, bias_ref, gq_ref, gk_ref, go_ref, o_ref,
                 kp_ref, acc_ref, m_ref, l_ref):
    i = pl.program_id(2)
    tq = q_ref.shape[1]
    seq = k_ref.shape[1]

    @pl.when(i == 0)
    def _prep_keys():
        def prep(c, carry):
            rows = _kv_rows(c, KPREP_ROWS)
            kn = _sub_rms(k_ref[0, rows, :].astype(F32), gk_ref[...])
            lo = lax.broadcasted_iota(jnp.int32, kn.shape, 1) < HALF_DIM
            kp_ref[0, rows, :] = jnp.where(lo, kn, 0.0).astype(BF16)
            kp_ref[1, rows, :] = jnp.where(lo, 0.0, kn).astype(BF16)
            return carry
        lax.fori_loop(0, seq // KPREP_ROWS, prep, 0)

    q = (_sub_rms(q_ref[0].astype(F32), gq_ref[...]) * (1.0 / math.sqrt(HALF_DIM))).astype(BF16)
    m_ref[...] = jnp.full_like(m_ref, NEG)
    l_ref[...] = jnp.zeros_like(l_ref)
    acc_ref[...] = jnp.zeros_like(acc_ref)

    def block(j, table):
        v = v_ref[0, _kv_rows(j, tq), :]
        for sub in range(2):
            e = lax.dot_general(q, kp_ref[sub, _kv_rows(j, tq), :], _NT, preferred_element_type=F32)
            if table is not None:
                e = e + bias_ref[0, table]
            _softmax_step(e, v, m_ref.at[sub], l_ref.at[sub], acc_ref.at[sub])

    block(i, 0)

    @pl.when(i >= 1)
    def _near():
        block(i - 1, 1)

    def body(step, carry):
        block(i - 2 - step, None)
        return carry

    lax.fori_loop(0, jnp.maximum(i - 1, 0), body, 0)
    o = acc_ref[0] / l_ref[0] - lam_ref[0] * (acc_ref[1] / l_ref[1])
    o_ref[...] = (_rms(o, go_ref[...]) * (1.0 - lam_init)).astype(BF16)


def _relative_bucket(rel):
    half = N_BUCKETS // 2
    max_exact = half // 2
    n = jnp.abs(rel)
    large = max_exact + (jnp.log(jnp.maximum(n, 1).astype(F32) / max_exact)
                         / math.log(MAX_DISTANCE / max_exact) * (half - max_exact)).astype(jnp.int32)
    large = jnp.minimum(large, half - 1)
    return jnp.where(rel > 0, half, 0) + jnp.where(n < max_exact, n, large)


def _diff_bias_tables(rel_bias, tq):
    assert tq >= MAX_DISTANCE and tq % CHUNK == 0
    t = jnp.arange(tq)[:, None]
    s = jnp.arange(tq)[None, :]
    far = rel_bias[N_BUCKETS // 2 - 1]
    tabs = []
    for delta in range(2):
        tab = rel_bias[_relative_bucket(s - t - delta * tq)] - far
        if delta == 0:
            tab = jnp.where(((s // CHUNK) <= (t // CHUNK))[:, :, None], tab, NEG)
        tabs.append(tab)
    return jnp.stack(tabs).transpose(3, 0, 1, 2).astype(F32)


def _diff_attention(proj, lam_val, lam_init, bias_tabs, gq, gk, go, batch, seq):
    t = proj.shape[1]
    tq = bias_tabs.shape[-1]
    nq = seq // tq
    base = 3 * H_SB
    vec = pl.BlockSpec((1, HEAD_DIM), lambda b, h, i: (0, 0))
    return pl.pallas_call(
        functools.partial(_diff_kernel, lam_init),
        grid=(batch, H_DIFF, nq),
        in_specs=[pl.BlockSpec(memory_space=pltpu.SMEM)]
        + _head_specs(nq, tq, seq, base, base + H_DIFF, base + 2 * H_DIFF)
        + [pl.BlockSpec((1, 2, tq, tq), lambda b, h, i: (h, 0, 0, 0)), vec, vec, vec],
        out_specs=pl.BlockSpec((tq, HEAD_DIM), lambda b, h, i: (b * nq + i, h)),
        out_shape=jax.ShapeDtypeStruct((t, H_DIFF * HEAD_DIM), BF16),
        scratch_shapes=[pltpu.VMEM((2, seq, HEAD_DIM), BF16), pltpu.VMEM((2, tq, HEAD_DIM), F32),
                        pltpu.VMEM((2, tq, 1), F32), pltpu.VMEM((2, tq, 1), F32)],
        compiler_params=_params(("arbitrary", "arbitrary", "arbitrary"), 40),
        name="diff_attention",
    )(lam_val.reshape(1), proj, proj, proj, bias_tabs, gq.reshape(1, HEAD_DIM), gk.reshape(1, HEAD_DIM),
      go.reshape(1, HEAD_DIM))


def _fox_kernel(q_ref, k_ref, v_ref, c_ref, gq_ref, gk_ref, go_ref, o_ref, kn_ref, acc_ref, m_ref, l_ref):
    i = pl.program_id(2)
    tq = q_ref.shape[1]
    seq = k_ref.shape[1]

    @pl.when(i == 0)
    def _prep_keys():
        def prep(c, carry):
            rows = _kv_rows(c, KPREP_ROWS)
            kn_ref[rows, :] = _rms(k_ref[0, rows, :].astype(F32), gk_ref[...]).astype(BF16)
            return carry
        lax.fori_loop(0, seq // KPREP_ROWS, prep, 0)

    q = (_rms(q_ref[0].astype(F32), gq_ref[...]) * (1.0 / math.sqrt(HEAD_DIM))).astype(BF16)
    row = lax.broadcasted_iota(jnp.int32, (tq, tq), 0)
    col = lax.broadcasted_iota(jnp.int32, (tq, tq), 1)
    causal = col <= row
    c_first = c_ref[0, i][:, 0:1]
    m_ref[...] = jnp.full_like(m_ref, NEG)
    l_ref[...] = jnp.zeros_like(l_ref)
    acc_ref[...] = jnp.zeros_like(acc_ref)

    def block(j, masked):
        e = lax.dot_general(q, kn_ref[_kv_rows(j, tq), :], _NT, preferred_element_type=F32)
        e = e + (c_first - c_ref[0, j])
        if masked:
            e = jnp.where(causal, e, NEG)
        _softmax_step(e, v_ref[0, _kv_rows(j, tq), :], m_ref, l_ref, acc_ref)

    block(i, True)

    def body(step, carry):
        block(i - 1 - step, False)
        return carry

    lax.fori_loop(0, i, body, 0)
    o_ref[...] = _rms(acc_ref[...] / l_ref[...], go_ref[...]).astype(BF16)


def _fox_attention(proj, cum, gq, gk, go, batch, seq):
    t = proj.shape[1]
    nq, tq = cum.shape[1], cum.shape[3]
    base = 3 * (H_SB + H_DIFF)
    vec = pl.BlockSpec((1, HEAD_DIM), lambda b, h, i: (0, 0))
    return pl.pallas_call(
        _fox_kernel,
        grid=(batch, H_FOX, nq),
        in_specs=_head_specs(nq, tq, seq, base, base + H_FOX, base + 2 * H_FOX)
        + [pl.BlockSpec((1, nq, 1, tq), lambda b, h, i: (b * GATE_ROWS + h, 0, 0, 0)), vec, vec, vec],
        out_specs=pl.BlockSpec((tq, HEAD_DIM), lambda b, h, i: (b * nq + i, h)),
        out_shape=jax.ShapeDtypeStruct((t, H_FOX * HEAD_DIM), BF16),
        scratch_shapes=[pltpu.VMEM((seq, HEAD_DIM), BF16), pltpu.VMEM((tq, HEAD_DIM), F32),
                        pltpu.VMEM((tq, 1), F32), pltpu.VMEM((tq, 1), F32)],
        compiler_params=_params(("arbitrary", "arbitrary", "arbitrary"), 40),
        name="fox_attention",
    )(proj, proj, proj, cum, gq.reshape(1, HEAD_DIM), gk.reshape(1, HEAD_DIM), go.reshape(1, HEAD_DIM))


def _out_proj_kernel(a_sb, a_diff, a_fox, w_sb, w_diff, w_fox, h_ref, o_ref):
    acc = jnp.dot(a_sb[...], w_sb[...], preferred_element_type=F32)
    acc += jnp.dot(a_diff[...], w_diff[...], preferred_element_type=F32)
    acc += jnp.dot(a_fox[...], w_fox[...], preferred_element_type=F32)
    o_ref[...] = h_ref[...] + acc


def _out_proj(heads, weights, h):
    t, d = h.shape
    tm, tn = _tile(t, OUT_TM), _tile(d, OUT_TN)
    return pl.pallas_call(
        _out_proj_kernel,
        grid=(t // tm, d // tn),
        in_specs=[pl.BlockSpec((tm, a.shape[1]), lambda i, j: (i, 0)) for a in heads]
        + [pl.BlockSpec((w.shape[0], tn), lambda i, j: (0, j)) for w in weights]
        + [pl.BlockSpec((tm, tn), lambda i, j: (i, j))],
        out_specs=pl.BlockSpec((tm, tn), lambda i, j: (i, j)),
        out_shape=jax.ShapeDtypeStruct((t, d), F32),
        compiler_params=_params(("arbitrary", "arbitrary"), 48),
        name="out_proj",
    )(*heads, *weights, h)


def _route_kernel(x_ref, g_ref, wh_ref, wl_ref, b_ref, eid_ref, gate_ref):
    y = _rms(x_ref[...], g_ref[...])
    p, _ = _small_proj_t(wh_ref[...], wl_ref[...], y)
    logits = p + b_ref[...]
    tm = logits.shape[1]

    def softmax0(a):
        ea = jnp.exp(a - jnp.max(a, axis=0, keepdims=True))
        return ea / jnp.sum(ea, axis=0, keepdims=True)

    def first_argmax(a, amax, n):
        rows = lax.broadcasted_iota(jnp.int32, a.shape, 0)
        return jnp.min(jnp.where(a == amax, rows, n), axis=0, keepdims=True)

    p_coarse = softmax0(logits[0:N_GROUPS])
    g_prob = jnp.max(p_coarse, axis=0, keepdims=True)
    g_idx = first_argmax(p_coarse, g_prob, N_GROUPS)
    fine = jnp.zeros((EXPERTS_PER_GROUP, tm), F32)
    for g in range(N_GROUPS):
        r0 = ROUTER_FINE_ROW0 + g * EXPERTS_PER_GROUP
        fine = jnp.where(g_idx == g, logits[r0:r0 + EXPERTS_PER_GROUP], fine)
    p_fine = softmax0(fine)
    p1 = jnp.max(p_fine, axis=0, keepdims=True)
    i1 = first_argmax(p_fine, p1, EXPERTS_PER_GROUP)
    rows = lax.broadcasted_iota(jnp.int32, p_fine.shape, 0)
    rest = jnp.where(rows == i1, -1.0, p_fine)
    p2 = jnp.max(rest, axis=0, keepdims=True)
    i2 = first_argmax(rest, p2, EXPERTS_PER_GROUP)
    denom = p1 + p2
    out_rows = lax.broadcasted_iota(jnp.int32, eid_ref.shape, 0)
    e1 = g_idx * EXPERTS_PER_GROUP + i1
    e2 = g_idx * EXPERTS_PER_GROUP + i2
    eid_ref[...] = jnp.where(out_rows == 0, e1, jnp.where(out_rows == 1, e2, 0))
    gate_ref[...] = jnp.where(out_rows == 0, g_prob * p1 / denom,
                              jnp.where(out_rows == 1, g_prob * p2 / denom, 0.0))


def _route(h, gain, w_hi, w_lo, bias):
    t, d = h.shape
    tm = _tile(t, NORM_TM)
    r = w_hi.shape[0]
    out_rows = 8
    return pl.pallas_call(
        _route_kernel,
        grid=(t // tm,),
        in_specs=[pl.BlockSpec((tm, d), lambda i: (i, 0)),
                  pl.BlockSpec((1, d), lambda i: (0, 0)),
                  pl.BlockSpec((r, d), lambda i: (0, 0)),
                  pl.BlockSpec((r, d), lambda i: (0, 0)),
                  pl.BlockSpec((r, 1), lambda i: (0, 0))],
        out_specs=[pl.BlockSpec((out_rows, tm), lambda i: (0, i)),
                   pl.BlockSpec((out_rows, tm), lambda i: (0, i))],
        out_shape=[jax.ShapeDtypeStruct((out_rows, t), jnp.int32),
                   jax.ShapeDtypeStruct((out_rows, t), F32)],
        compiler_params=_params(("arbitrary",), 40),
        name="route",
    )(h, gain.reshape(1, d), w_hi, w_lo, bias)


def _row_copy(src_hbm, row, dst, slot, sem):
    return pltpu.make_async_copy(src_hbm.at[pl.ds(row, 1)], dst.at[pl.ds(slot, 1)], sem)


def _gather_kernel(tok_ref, h_hbm, o_ref, sem):
    tm = o_ref.shape[0]

    def start(r, carry):
        _row_copy(h_hbm, tok_ref[0, 0, r], o_ref, r, sem).start()
        return carry

    def wait(r, carry):
        _row_copy(h_hbm, tok_ref[0, 0, r], o_ref, r, sem).wait()
        return carry

    lax.fori_loop(0, tm, start, 0)
    lax.fori_loop(0, tm, wait, 0)


def _gather(h, row_tok):
    n_blocks, _, tm = row_tok.shape
    d = h.shape[1]
    return pl.pallas_call(
        _gather_kernel,
        grid=(n_blocks,),
        in_specs=[pl.BlockSpec((1, 1, tm), lambda i: (i, 0, 0), memory_space=pltpu.SMEM),
                  pl.BlockSpec(memory_space=pl.ANY)],
        out_specs=pl.BlockSpec((tm, d), lambda i: (i, 0)),
        out_shape=jax.ShapeDtypeStruct((n_blocks * tm, d), F32),
        scratch_shapes=[pltpu.SemaphoreType.DMA(())],
        compiler_params=_params(("arbitrary",), 32),
        name="moe_gather",
    )(row_tok, h)


def _experts_kernel(be_ref, nu_ref, x_ref, g_ref, wg_ref, wu_ref, wd_ref, o_ref):
    @pl.when(pl.program_id(0) < nu_ref[0])
    def _run():
        half = o_ref.shape[1]
        x = _rms(x_ref[...], g_ref[...]).astype(BF16)
        a = jnp.dot(x, wg_ref[0], preferred_element_type=F32)
        b = jnp.dot(x, wu_ref[0], preferred_element_type=F32)
        hid = (a / (1.0 + jnp.exp(-a)) * b).astype(BF16)
        y = jnp.dot(hid, wd_ref[0], preferred_element_type=F32)
        hi = lax.bitcast_convert_type(y[:, :half].astype(BF16).astype(F32), jnp.uint32)
        lo = lax.bitcast_convert_type(y[:, half:].astype(BF16).astype(F32), jnp.uint32)
        o_ref[...] = (hi & jnp.uint32(0xFFFF0000)) | (lo >> 16)

    @pl.when(pl.program_id(0) >= nu_ref[0])
    def _unused_tail():
        o_ref[...] = jnp.zeros_like(o_ref)


def _experts(xs, gain, block_e, n_used, wg, wu, wd, tm):
    n_rows, d = xs.shape
    f = wg.shape[2]
    n_blocks = n_rows // tm

    def row_map(i, be, nu):
        return (jnp.minimum(i, nu[0] - 1), 0)

    def w_map(i, be, nu):
        return (be[jnp.minimum(i, nu[0] - 1)], 0, 0)

    return pl.pallas_call(
        _experts_kernel,
        grid_spec=pltpu.PrefetchScalarGridSpec(
            num_scalar_prefetch=2,
            grid=(n_blocks,),
            in_specs=[pl.BlockSpec((tm, d), row_map),
                      pl.BlockSpec((1, d), lambda i, be, nu: (0, 0)),
                      pl.BlockSpec((1, d, f), w_map),
                      pl.BlockSpec((1, d, f), w_map),
                      pl.BlockSpec((1, f, d), w_map)],
            out_specs=pl.BlockSpec((tm, d // 2), lambda i, be, nu: (i, 0))),
        out_shape=jax.ShapeDtypeStruct((n_rows, d // 2), jnp.uint32),
        compiler_params=_params(("arbitrary",), 48),
        name="moe_experts",
    )(block_e, n_used, xs, gain.reshape(1, d), wg, wu, wd)


def _combine_kernel(d_ref, g_ref, h_ref, ys_hbm, o_ref, buf, sem):
    tc, half = buf.shape[1], buf.shape[2]

    def start(r, carry):
        for k in range(TOP_K):
            _row_copy(ys_hbm, d_ref[0, 0, k * tc + r], buf.at[k], r, sem).start()
        return carry

    def wait(r, carry):
        for k in range(TOP_K):
            _row_copy(ys_hbm, d_ref[0, 0, k * tc + r], buf.at[k], r, sem).wait()
        return carry

    lax.fori_loop(0, tc, start, 0)
    lax.fori_loop(0, tc, wait, 0)
    g = g_ref[...]
    y_hi = jnp.zeros((tc, half), F32)
    y_lo = jnp.zeros((tc, half), F32)
    for k in range(TOP_K):
        w = buf[k]
        gk = g[:, k:k + 1]
        y_hi += gk * lax.bitcast_convert_type(w & jnp.uint32(0xFFFF0000), F32)
        y_lo += gk * lax.bitcast_convert_type(w << 16, F32)
    o_ref[:, :half] = h_ref[:, :half] + y_hi
    o_ref[:, half:] = h_ref[:, half:] + y_lo


def _combine(h, ys, dest, gates):
    t, d = h.shape
    n_tb, _, two_tc = dest.shape
    tc = two_tc // TOP_K
    return pl.pallas_call(
        _combine_kernel,
        grid=(n_tb,),
        in_specs=[pl.BlockSpec((1, 1, two_tc), lambda i: (i, 0, 0), memory_space=pltpu.SMEM),
                  pl.BlockSpec((tc, TOP_K), lambda i: (i, 0)),
                  pl.BlockSpec((tc, d), lambda i: (i, 0)),
                  pl.BlockSpec(memory_space=pl.ANY)],
        out_specs=pl.BlockSpec((tc, d), lambda i: (i, 0)),
        out_shape=jax.ShapeDtypeStruct((t, d), F32),
        scratch_shapes=[pltpu.VMEM((TOP_K, tc, d // 2), jnp.uint32), pltpu.SemaphoreType.DMA(())],
        compiler_params=_params(("arbitrary",), 40),
        name="moe_combine",
    )(dest, gates, h, ys)


def _dispatch_plan(eid, t, tm, tc):
    flat_e = eid[:TOP_K].reshape(-1)
    onehot = (flat_e[:, None] == jnp.arange(N_EXPERTS, dtype=jnp.int32)[None, :]).astype(jnp.int32)
    csum = jnp.cumsum(onehot, axis=0)
    counts = csum[-1]
    rank = jnp.sum(onehot * csum, axis=1) - 1
    padded = (counts + tm - 1) // tm * tm
    pends = jnp.cumsum(padded)
    pstarts = pends - padded
    dest = jnp.sum(onehot * pstarts[None, :], axis=1) + rank
    n_blocks = -(-(TOP_K * t + N_EXPERTS * (tm - 1)) // tm)
    flat_tok = jnp.tile(jnp.arange(t, dtype=jnp.int32), TOP_K)
    row_tok = jnp.zeros((n_blocks * tm,), jnp.int32).at[dest].set(flat_tok, unique_indices=True)
    starts = jnp.arange(n_blocks, dtype=jnp.int32) * tm
    block_e = jnp.minimum(jnp.sum((starts[:, None] >= pends[None, :]).astype(jnp.int32), axis=1),
                          N_EXPERTS - 1).astype(jnp.int32)
    n_used = (pends[-1] // tm).astype(jnp.int32).reshape(1)
    dest_blocks = dest.reshape(TOP_K, t // tc, tc).transpose(1, 0, 2).reshape(t // tc, 1, TOP_K * tc)
    return row_tok.reshape(n_blocks, 1, tm), block_e, n_used, dest_blocks.astype(jnp.int32)


def _hi_lo_rows(w_t, rows):
    padded = jnp.zeros((rows, w_t.shape[1]), F32).at[:w_t.shape[0]].set(w_t)
    return _split_bf16(padded)


def _token_mixers(h, layer, bias_tabs, norm_attn, w_in, b_f, q_gain_diff, k_gain_diff, lam, subln_gain,
                  q_gain_fox, k_gain_fox, out_gain_sb, out_gain_fox, w_out, batch, seq):
    n_qkv = 3 * D_MODEL
    wf_hi, wf_lo = _hi_lo_rows(w_in[:, n_qkv:].T, GATE_ROWS)
    u, fc_t = _attn_norm(h, norm_attn, wf_hi, wf_lo)
    proj = _in_proj(u, w_in[:, :n_qkv].astype(BF16))

    b_pad = jnp.zeros((GATE_ROWS, 1), F32).at[:H_FOX, 0].set(b_f)
    cum_t = _gate_cumsum(fc_t, b_pad, batch)
    tq = bias_tabs.shape[-1]
    nq = seq // tq
    cum = cum_t.reshape(GATE_ROWS, batch, nq, 1, tq).transpose(1, 0, 2, 3, 4).reshape(batch * GATE_ROWS, nq, 1, tq)

    lam_init = 0.8 - 0.6 * math.exp(-0.3 * layer)
    lamf = lam.astype(F32)
    lam_val = jnp.exp(jnp.sum(lamf[0] * lamf[1])) - jnp.exp(jnp.sum(lamf[2] * lamf[3])) + lam_init

    o_sb = _sb_attention(proj, out_gain_sb, batch, seq)
    o_diff = _diff_attention(proj, lam_val, lam_init, bias_tabs, q_gain_diff.reshape(-1),
                             k_gain_diff.reshape(-1), subln_gain, batch, seq)
    o_fox = _fox_attention(proj, cum, q_gain_fox, k_gain_fox, out_gain_fox, batch, seq)

    w_out = w_out.astype(BF16)
    c1, c2 = H_SB * HEAD_DIM, (H_SB + H_DIFF) * HEAD_DIM
    return _out_proj([o_sb, o_diff, o_fox], [w_out[:c1], w_out[c1:c2], w_out[c2:]], h)


def _moe(h, norm_ffn, w_coarse, b_coarse, w_fine, b_fine, w_gate, w_up, w_down):
    t, d = h.shape
    assert EXPERTS_PER_GROUP == 8 and N_GROUPS <= ROUTER_FINE_ROW0 and TOP_K == 2
    w_router = jnp.zeros((ROUTER_ROWS, d), F32)
    w_router = w_router.at[:N_GROUPS].set(w_coarse.T)
    w_router = w_router.at[ROUTER_FINE_ROW0:ROUTER_FINE_ROW0 + N_EXPERTS].set(w_fine.T)
    b_router = jnp.zeros((ROUTER_ROWS, 1), F32)
    b_router = b_router.at[:N_GROUPS, 0].set(b_coarse)
    b_router = b_router.at[ROUTER_FINE_ROW0:ROUTER_FINE_ROW0 + N_EXPERTS, 0].set(b_fine)
    wr_hi, wr_lo = _split_bf16(w_router)
    eid, gate = _route(h, norm_ffn, wr_hi, wr_lo, b_router)

    tm = MOE_TM
    tc = _tile(t, COMBINE_TC, 8)
    row_tok, block_e, n_used, dest = _dispatch_plan(eid, t, tm, tc)
    xs = _gather(h, row_tok)
    ys = _experts(xs, norm_ffn, block_e, n_used, w_gate.astype(BF16), w_up.astype(BF16),
                  w_down.astype(BF16), tm)
    return _combine(h, ys, dest, gate[:TOP_K].T)


def kernel(x, norm_attn, w_in, b_f, q_gain_diff, k_gain_diff, lam, subln_gain, q_gain_fox, k_gain_fox,
           out_gain_sb, out_gain_fox, rel_bias, w_out, norm_ffn, w_coarse, b_coarse, w_fine, b_fine,
           w_gate, w_up, w_down):
    batch, seq, d = x.shape
    h = x.reshape(batch * seq, d)
    bias_tabs = _diff_bias_tables(rel_bias, _tile(seq, ATT_T))
    for layer in range(DEPTH):
        h = _token_mixers(h, layer, bias_tabs, norm_attn[layer], w_in[layer], b_f[layer],
                          q_gain_diff[layer], k_gain_diff[layer], lam[layer], subln_gain[layer],
                          q_gain_fox[layer], k_gain_fox[layer], out_gain_sb[layer], out_gain_fox[layer],
                          w_out[layer], batch, seq)
        h = _moe(h, norm_ffn[layer], w_coarse[layer], b_coarse[layer], w_fine[layer], b_fine[layer],
                 w_gate[layer], w_up[layer], w_down[layer])
    return h.reshape(batch, seq, d)
```

```python
import functools
import math

import jax
import jax.numpy as jnp
from jax import lax
from jax.experimental import pallas as pl
from jax.experimental.pallas import tpu as pltpu

D_MODEL = 4096
BATCH = 2
SEQ = 8192
DEPTH = 2
CHUNK = 64
HEAD_DIM = 128
HALF_DIM = HEAD_DIM // 2
N_HEADS = D_MODEL // HEAD_DIM
H_DIFF = (N_HEADS * 5) // 16
H_FOX = (N_HEADS * 5) // 16
H_SB = N_HEADS - H_DIFF - H_FOX
N_BUCKETS = 32
MAX_DISTANCE = 128
N_GROUPS = 4
EXPERTS_PER_GROUP = 8
N_EXPERTS = N_GROUPS * EXPERTS_PER_GROUP
TOP_K = 2
D_EXPERT = 512
EPS = 1e-6

F32 = jnp.float32
BF16 = jnp.bfloat16
NEG = -1e30
LOG2E = math.log2(math.e)
SB_UNDERFLOW = 104.0
SAFE_EXP2_RANGE = 64.0
SUBLANES_BF16 = 16
LANES = 128
GATE_ROWS = 16
ROUTER_FINE_ROW0 = 8
ROUTER_ROWS = 48

NORM_TM = 512
PROJ_TM = 1024
PROJ_TN = 512
ATT_T = 512
KPREP_ROWS = 512
OUT_TM = 1024
OUT_TN = 512
MOE_TM = 256
DMA_QUEUES = 2
COMBINE_TC = 256

_NT = (((1,), (1,)), ((), ()))


def _params(semantics, vmem_mib):
    return pltpu.CompilerParams(dimension_semantics=semantics, vmem_limit_bytes=vmem_mib << 20)


def _tile(n, pref, unit=LANES):
    t = min(pref, n) // unit * unit
    while n % t:
        t -= unit
    return t


def _rms(x, gain):
    return x * lax.rsqrt(jnp.mean(x * x, axis=-1, keepdims=True) + EPS) * gain


def _split_bf16(a):
    hi = a.astype(BF16)
    lo = (a - hi.astype(F32)).astype(BF16)
    return hi, lo


def _small_proj_t(w_hi, w_lo, y):
    y_hi, y_lo = _split_bf16(y)
    p = lax.dot_general(w_hi, y_hi, _NT, preferred_element_type=F32)
    p += lax.dot_general(w_lo, y_hi, _NT, preferred_element_type=F32)
    p += lax.dot_general(w_hi, y_lo, _NT, preferred_element_type=F32)
    return p, y_hi


def _attn_norm_kernel(x_ref, g_ref, wh_ref, wl_ref, u_ref, p_ref):
    y = _rms(x_ref[...], g_ref[...])
    p, y_hi = _small_proj_t(wh_ref[...], wl_ref[...], y)
    u_ref[...] = y_hi
    p_ref[...] = p


def _attn_norm(h, gain, w_hi, w_lo):
    t, d = h.shape
    tm = _tile(t, NORM_TM)
    r = w_hi.shape[0]
    return pl.pallas_call(
        _attn_norm_kernel,
        grid=(t // tm,),
        in_specs=[pl.BlockSpec((tm, d), lambda i: (i, 0)),
                  pl.BlockSpec((1, d), lambda i: (0, 0)),
                  pl.BlockSpec((r, d), lambda i: (0, 0)),
                  pl.BlockSpec((r, d), lambda i: (0, 0))],
        out_specs=[pl.BlockSpec((tm, d), lambda i: (i, 0)),
                   pl.BlockSpec((r, tm), lambda i: (0, i))],
        out_shape=[jax.ShapeDtypeStruct((t, d), BF16), jax.ShapeDtypeStruct((r, t), F32)],
        compiler_params=_params(("arbitrary",), 40),
        name="attn_norm",
    )(h, gain.reshape(1, d), w_hi, w_lo)


def _in_proj_kernel(u_ref, w_ref, o_ref, wb_ref):
    @pl.when(pl.program_id(1) == 0)
    def _cast_weights():
        wb_ref[...] = w_ref[...].astype(BF16)

    acc = jnp.dot(u_ref[...], wb_ref[...], preferred_element_type=F32)
    for s in range(o_ref.shape[0]):
        o_ref[s] = acc[:, s * HEAD_DIM:(s + 1) * HEAD_DIM].astype(BF16)


def _in_proj(u, w, n):
    t, d = u.shape
    tm, tn = _tile(t, PROJ_TM), _tile(n, PROJ_TN)
    return pl.pallas_call(
        _in_proj_kernel,
        grid=(n // tn, t // tm),
        in_specs=[pl.BlockSpec((tm, d), lambda j, i: (i, 0)),
                  pl.BlockSpec((d, tn), lambda j, i: (0, j))],
        out_specs=pl.BlockSpec((tn // HEAD_DIM, tm, HEAD_DIM), lambda j, i: (j, i, 0)),
        out_shape=jax.ShapeDtypeStruct((n // HEAD_DIM, t, HEAD_DIM), BF16),
        scratch_shapes=[pltpu.VMEM((d, tn), BF16)],
        compiler_params=_params(("arbitrary", "arbitrary"), 48),
        name="in_proj",
    )(u, w)


def _gate_cumsum_kernel(fc_ref, b_ref, tri_ref, o_ref):
    x = fc_ref[...] + b_ref[...]
    log_f = jnp.minimum(x, 0.0) - jnp.log(1.0 + jnp.exp(-jnp.abs(x)))
    tri = tri_ref[...]
    carry = jnp.zeros((x.shape[0], 1), F32)
    for c in range(x.shape[1] // LANES):
        seg = log_f[:, c * LANES:(c + 1) * LANES]
        a = seg.astype(BF16)
        rest = seg - a.astype(F32)
        b, c3 = _split_bf16(rest)
        cs = (jnp.dot(a, tri, preferred_element_type=F32) + jnp.dot(b, tri, preferred_element_type=F32)
              + jnp.dot(c3, tri, preferred_element_type=F32)) + carry
        o_ref[:, c * LANES:(c + 1) * LANES] = cs
        carry = cs[:, LANES - 1:LANES]


def _gate_cumsum(fc_t, b_f, batch):
    r, t = fc_t.shape
    s = t // batch
    idx = jnp.arange(LANES)
    tri = (idx[:, None] <= idx[None, :]).astype(BF16)
    return pl.pallas_call(
        _gate_cumsum_kernel,
        grid=(batch,),
        in_specs=[pl.BlockSpec((r, s), lambda b: (0, b)),
                  pl.BlockSpec((r, 1), lambda b: (0, 0)),
                  pl.BlockSpec((LANES, LANES), lambda b: (0, 0))],
        out_specs=pl.BlockSpec((r, s), lambda b: (0, b)),
        out_shape=jax.ShapeDtypeStruct((r, t), F32),
        compiler_params=_params(("arbitrary",), 32),
        name="gate_cumsum",
    )(fc_t, b_f, tri)


def _kv_rows(j, tk):
    return pl.ds(pl.multiple_of(j * tk, tk), tk)


def _with_ones(v):
    return jnp.concatenate([v, jnp.ones_like(v)], axis=1)


def _row_max(e):
    part = e[:, 0:LANES]
    for c in range(1, e.shape[1] // LANES):
        part = jnp.maximum(part, e[:, c * LANES:(c + 1) * LANES])
    return jnp.max(part, axis=-1, keepdims=True)


def _softmax_first(e, vx, m_ref, acc_ref):
    m = _row_max(e)
    m_ref[...] = jnp.broadcast_to(m, m_ref.shape)
    acc_ref[...] = jnp.dot(jnp.exp2(e - m).astype(BF16), vx, preferred_element_type=F32)


def _softmax_fixed(e, vx, m_ref, acc_ref):
    m = m_ref[...]
    p = jnp.concatenate([jnp.exp2(e[:, c * LANES:(c + 1) * LANES] - m)
                         for c in range(e.shape[1] // LANES)], axis=1)
    acc_ref[...] += jnp.dot(p.astype(BF16), vx, preferred_element_type=F32)


def _softmax_online(e, vx, m_ref, acc_ref):
    m_old = m_ref[:, 0:1]
    m_new = jnp.maximum(m_old, _row_max(e))
    alpha = jnp.exp2(m_old - m_new)
    p = jnp.exp2(e - m_new).astype(BF16)
    acc_ref[...] = alpha * acc_ref[...] + jnp.dot(p, vx, preferred_element_type=F32)
    m_ref[...] = jnp.broadcast_to(m_new, m_ref.shape)


def _sweep_left(i, first_step, safe, step_fn):
    count = jnp.maximum(i - first_step + 1, 0)

    @pl.when(safe)
    def _fixed():
        def pair(step, carry):
            step_fn(i - first_step - 1 - 2 * step, 2, _softmax_fixed)
            return carry
        lax.fori_loop(0, lax.shift_right_logical(count, 1), pair, 0)

        @pl.when((count & 1) == 1)
        def _last():
            step_fn(0, 1, _softmax_fixed)

    @pl.when(jnp.logical_not(safe))
    def _online():
        def single(step, carry):
            step_fn(i - first_step - step, 1, _softmax_online)
            return carry
        lax.fori_loop(0, count, single, 0)


def _softmax_result(acc_ref):
    acc = acc_ref[...]
    return acc[:, :HEAD_DIM] / acc[:, HEAD_DIM:]


def _head_specs(t_blocks_per_batch, tq, seq, q_slot, k_slot, v_slot):
    nq = t_blocks_per_batch
    return [pl.BlockSpec((1, tq, HEAD_DIM), lambda b, h, i: (q_slot + h, b * nq + i, 0)),
            pl.BlockSpec((1, seq, HEAD_DIM), lambda b, h, i: (k_slot + h, b, 0)),
            pl.BlockSpec((1, seq, HEAD_DIM), lambda b, h, i: (v_slot + h, b, 0))]


def _sb_kernel(q_ref, k_ref, v_ref, tri_ref, g_ref, o_ref, acc_ref, run_ref):
    i = pl.program_id(2)
    tq = q_ref.shape[1]
    q = (q_ref[0].astype(F32) * (1.0 / math.sqrt(HEAD_DIM))).astype(BF16)
    row = lax.broadcasted_iota(jnp.int32, (tq, tq), 0)
    col = lax.broadcasted_iota(jnp.int32, (tq, tq), 1)
    strict = col < row
    acc_ref[...] = jnp.zeros_like(acc_ref)
    run_ref[...] = jnp.zeros_like(run_ref)

    def block(j, masked):
        k = k_ref[0, _kv_rows(j, tq), :]
        v = v_ref[0, _kv_rows(j, tq), :]
        z = lax.dot_general(q, k, _NT, preferred_element_type=F32)
        sp = jnp.maximum(z, 0.0) + jnp.log(1.0 + jnp.exp(-jnp.abs(z)))
        if masked:
            sp = jnp.where(strict, sp, 0.0)
        suffix = jnp.dot(sp.astype(BF16), tri_ref[...], preferred_element_type=F32)
        w = jnp.exp(z - suffix - run_ref[...])
        if masked:
            w = jnp.where(strict, w, 0.0)
        acc_ref[...] += jnp.dot(w.astype(BF16), v, preferred_element_type=F32)
        run_ref[...] += suffix[:, 0:1]

    block(i, True)

    def more(carry):
        step, smallest_run = carry
        return jnp.logical_and(step < i, smallest_run < SB_UNDERFLOW)

    def body(carry):
        step, _ = carry
        block(i - 1 - step, False)
        return step + 1, jnp.min(run_ref[...])

    lax.while_loop(more, body, (0, jnp.min(run_ref[...])))
    o_ref[...] = _rms(acc_ref[...], g_ref[...]).astype(BF16)


def _sb_attention(proj, gain, batch, seq):
    t = proj.shape[1]
    tq = _tile(seq, ATT_T)
    nq = seq // tq
    idx = jnp.arange(tq)
    tri = (idx[:, None] >= idx[None, :]).astype(BF16)
    return pl.pallas_call(
        _sb_kernel,
        grid=(batch, H_SB, nq),
        in_specs=_head_specs(nq, tq, seq, 0, H_SB, 2 * H_SB) + [
            pl.BlockSpec((tq, tq), lambda b, h, i: (0, 0)),
            pl.BlockSpec((1, HEAD_DIM), lambda b, h, i: (0, 0))],
        out_specs=pl.BlockSpec((tq, HEAD_DIM), lambda b, h, i: (b * nq + i, h)),
        out_shape=jax.ShapeDtypeStruct((t, H_SB * HEAD_DIM), BF16),
        scratch_shapes=[pltpu.VMEM((tq, HEAD_DIM), F32), pltpu.VMEM((tq, 1), F32)],
        compiler_params=_params(("arbitrary", "arbitrary", "arbitrary"), 40),
        name="sb_attention",
    )(proj, proj, proj, tri, gain.reshape(1, HEAD_DIM))


def _sub_rms(x, gain):
    lo = lax.broadcasted_iota(jnp.int32, x.shape, 1) < HALF_DIM
    sq = x * x
    ms_lo = jnp.sum(jnp.where(lo, sq, 0.0), axis=-1, keepdims=True) * (1.0 / HALF_DIM)
    ms_hi = jnp.sum(jnp.where(lo, 0.0, sq), axis=-1, keepdims=True) * (1.0 / HALF_DIM)
    inv = jnp.where(lo, lax.rsqrt(ms_lo + EPS), lax.rsqrt(ms_hi + EPS))
    return x * inv * gain


def _diff_kernel(lam_init, lam_ref, safe_ref, q_ref, k_ref, v_ref, bias_ref, gq_ref, gk_ref, go_ref, o_ref,
                 kp_ref, acc_ref, m_ref):
    i = pl.program_id(2)
    tq = q_ref.shape[1]
    seq = k_ref.shape[1]

    @pl.when(i == 0)
    def _prep_keys():
        def prep(c, carry):
            rows = _kv_rows(c, KPREP_ROWS)
            kn = _sub_rms(k_ref[0, rows, :].astype(F32), gk_ref[...])
            lo = lax.broadcasted_iota(jnp.int32, kn.shape, 1) < HALF_DIM
            kp_ref[0, rows, :] = jnp.where(lo, kn, 0.0).astype(BF16)
            kp_ref[1, rows, :] = jnp.where(lo, 0.0, kn).astype(BF16)
            return carry
        lax.fori_loop(0, seq // KPREP_ROWS, prep, 0)

    q = (_sub_rms(q_ref[0].astype(F32), gq_ref[...]) * (LOG2E / math.sqrt(HALF_DIM))).astype(BF16)
    safe = safe_ref[pl.program_id(1)] != 0

    def block(j, n, table, update):
        rows = pl.ds(pl.multiple_of(j * tq, tq), n * tq)
        vx = _with_ones(v_ref[0, rows, :])
        for sub in range(2):
            e = lax.dot_general(q, kp_ref[sub, rows, :], _NT, preferred_element_type=F32)
            if table is not None:
                e = e + bias_ref[0, table]
            update(e, vx, m_ref.at[sub], acc_ref.at[sub])

    block(i, 1, 0, _softmax_first)
    for flag, update in ((safe, _softmax_fixed), (jnp.logical_not(safe), _softmax_online)):
        @pl.when(jnp.logical_and(i >= 1, flag))
        def _near(update=update):
            block(i - 1, 1, 1, update)
    _sweep_left(i, 2, safe, lambda j, n, update: block(j, n, None, update))
    o = _softmax_result(acc_ref.at[0]) - lam_ref[0] * _softmax_result(acc_ref.at[1])
    o_ref[...] = (_rms(o, go_ref[...]) * (1.0 - lam_init)).astype(BF16)


def _relative_bucket(rel):
    half = N_BUCKETS // 2
    max_exact = half // 2
    n = jnp.abs(rel)
    large = max_exact + (jnp.log(jnp.maximum(n, 1).astype(F32) / max_exact)
                         / math.log(MAX_DISTANCE / max_exact) * (half - max_exact)).astype(jnp.int32)
    large = jnp.minimum(large, half - 1)
    return jnp.where(rel > 0, half, 0) + jnp.where(n < max_exact, n, large)


def _diff_bias_tables(rel_bias, tq):
    assert tq >= MAX_DISTANCE and tq % CHUNK == 0
    t = jnp.arange(tq)[:, None]
    s = jnp.arange(tq)[None, :]
    rel = jnp.stack([s - t - delta * tq for delta in range(2)])
    bucket = _relative_bucket(rel)[None]
    shifted = ((rel_bias - rel_bias[N_BUCKETS // 2 - 1]) * LOG2E).astype(F32)
    tab = jnp.zeros((rel_bias.shape[1], 2, tq, tq), F32)
    for b in range(N_BUCKETS):
        tab = jnp.where(bucket == b, shifted[b][:, None, None, None], tab)
    visible = jnp.stack([(s // CHUNK) <= (t // CHUNK), jnp.ones((tq, tq), bool)])
    return jnp.where(visible[None], tab, NEG)


def _diff_attention(proj, lam_val, lam_init, bias_tabs, rel_bias, gq, gk, go, batch, seq):
    t = proj.shape[1]
    tq = bias_tabs.shape[-1]
    nq = seq // tq
    base = 3 * H_SB
    vec = pl.BlockSpec((1, HEAD_DIM), lambda b, h, i: (0, 0))
    bound = 2.0 * math.sqrt(HALF_DIM) * jnp.max(jnp.abs(gq)) * jnp.max(jnp.abs(gk))
    spread = jnp.max(rel_bias, axis=0) - jnp.min(rel_bias, axis=0)
    safe = ((bound + spread) * LOG2E < SAFE_EXP2_RANGE).astype(jnp.int32)
    return pl.pallas_call(
        functools.partial(_diff_kernel, lam_init),
        grid=(batch, H_DIFF, nq),
        in_specs=[pl.BlockSpec(memory_space=pltpu.SMEM), pl.BlockSpec(memory_space=pltpu.SMEM)]
        + _head_specs(nq, tq, seq, base, base + H_DIFF, base + 2 * H_DIFF)
        + [pl.BlockSpec((1, 2, tq, tq), lambda b, h, i: (h, 0, 0, 0)), vec, vec, vec],
        out_specs=pl.BlockSpec((tq, HEAD_DIM), lambda b, h, i: (b * nq + i, h)),
        out_shape=jax.ShapeDtypeStruct((t, H_DIFF * HEAD_DIM), BF16),
        scratch_shapes=[pltpu.VMEM((2, seq, HEAD_DIM), BF16), pltpu.VMEM((2, tq, 2 * HEAD_DIM), F32),
                        pltpu.VMEM((2, tq, LANES), F32)],
        compiler_params=_params(("arbitrary", "arbitrary", "arbitrary"), 48),
        name="diff_attention",
    )(lam_val.reshape(1), safe, proj, proj, proj, bias_tabs, gq.reshape(1, HEAD_DIM), gk.reshape(1, HEAD_DIM),
      go.reshape(1, HEAD_DIM))


def _fox_kernel(safe_ref, q_ref, k_ref, v_ref, c_ref, gq_ref, gk_ref, go_ref, o_ref, kn_ref, acc_ref, m_ref):
    i = pl.program_id(2)
    tq = q_ref.shape[1]
    seq = k_ref.shape[1]

    @pl.when(i == 0)
    def _prep_keys():
        def prep(c, carry):
            rows = _kv_rows(c, KPREP_ROWS)
            kn_ref[rows, :] = _rms(k_ref[0, rows, :].astype(F32), gk_ref[...]).astype(BF16)
            return carry
        lax.fori_loop(0, seq // KPREP_ROWS, prep, 0)

    q = (_rms(q_ref[0].astype(F32), gq_ref[...]) * (LOG2E / math.sqrt(HEAD_DIM))).astype(BF16)
    row = lax.broadcasted_iota(jnp.int32, (tq, tq), 0)
    col = lax.broadcasted_iota(jnp.int32, (tq, tq), 1)
    causal = col <= row
    c_first = c_ref[0, i][:, 0:1]

    def block(j, n, masked, update):
        rows = pl.ds(pl.multiple_of(j * tq, tq), n * tq)
        e = lax.dot_general(q, kn_ref[rows, :], _NT, preferred_element_type=F32)
        c_keys = jnp.concatenate([c_ref[0, j + b] for b in range(n)], axis=1)
        e = e + (c_first - c_keys) * LOG2E
        if masked:
            e = jnp.where(causal, e, NEG)
        update(e, _with_ones(v_ref[0, rows, :]), m_ref, acc_ref)

    block(i, 1, True, _softmax_first)
    _sweep_left(i, 1, safe_ref[0] != 0, lambda j, n, update: block(j, n, False, update))
    o_ref[...] = _rms(_softmax_result(acc_ref), go_ref[...]).astype(BF16)


def _fox_attention(proj, cum, gq, gk, go, batch, seq):
    t = proj.shape[1]
    nq, tq = cum.shape[1], cum.shape[3]
    base = 3 * (H_SB + H_DIFF)
    vec = pl.BlockSpec((1, HEAD_DIM), lambda b, h, i: (0, 0))
    bound = 2.0 * math.sqrt(HEAD_DIM) * jnp.max(jnp.abs(gq)) * jnp.max(jnp.abs(gk))
    safe = (bound * LOG2E < SAFE_EXP2_RANGE).astype(jnp.int32).reshape(1)
    return pl.pallas_call(
        _fox_kernel,
        grid=(batch, H_FOX, nq),
        in_specs=[pl.BlockSpec(memory_space=pltpu.SMEM)]
        + _head_specs(nq, tq, seq, base, base + H_FOX, base + 2 * H_FOX)
        + [pl.BlockSpec((1, nq, 1, tq), lambda b, h, i: (b * GATE_ROWS + h, 0, 0, 0)), vec, vec, vec],
        out_specs=pl.BlockSpec((tq, HEAD_DIM), lambda b, h, i: (b * nq + i, h)),
        out_shape=jax.ShapeDtypeStruct((t, H_FOX * HEAD_DIM), BF16),
        scratch_shapes=[pltpu.VMEM((seq, HEAD_DIM), BF16), pltpu.VMEM((tq, 2 * HEAD_DIM), F32),
                        pltpu.VMEM((tq, LANES), F32)],
        compiler_params=_params(("arbitrary", "arbitrary", "arbitrary"), 40),
        name="fox_attention",
    )(safe, proj, proj, proj, cum, gq.reshape(1, HEAD_DIM), gk.reshape(1, HEAD_DIM), go.reshape(1, HEAD_DIM))


def _out_proj_kernel(a_sb, a_diff, a_fox, w_ref, h_ref, o_ref, wb_ref):
    @pl.when(pl.program_id(1) == 0)
    def _cast_weights():
        wb_ref[...] = w_ref[...].astype(BF16)

    acc = h_ref[...]
    row = 0
    for a in (a_sb, a_diff, a_fox):
        width = a.shape[1]
        acc += jnp.dot(a[...], wb_ref[row:row + width, :], preferred_element_type=F32)
        row += width
    o_ref[...] = acc


def _out_proj(heads, w, h):
    t, d = h.shape
    tm, tn = _tile(t, OUT_TM), _tile(d, OUT_TN)
    return pl.pallas_call(
        _out_proj_kernel,
        grid=(d // tn, t // tm),
        in_specs=[pl.BlockSpec((tm, a.shape[1]), lambda j, i: (i, 0)) for a in heads]
        + [pl.BlockSpec((d, tn), lambda j, i: (0, j)),
           pl.BlockSpec((tm, tn), lambda j, i: (i, j))],
        out_specs=pl.BlockSpec((tm, tn), lambda j, i: (i, j)),
        out_shape=jax.ShapeDtypeStruct((t, d), F32),
        scratch_shapes=[pltpu.VMEM((d, tn), BF16)],
        compiler_params=_params(("arbitrary", "arbitrary"), 52),
        name="out_proj",
    )(*heads, w, h)


def _route_kernel(x_ref, g_ref, wh_ref, wl_ref, b_ref, eid_ref, gate_ref):
    y = _rms(x_ref[...], g_ref[...])
    p, _ = _small_proj_t(wh_ref[...], wl_ref[...], y)
    logits = p + b_ref[...]
    tm = logits.shape[1]

    def softmax0(a):
        ea = jnp.exp(a - jnp.max(a, axis=0, keepdims=True))
        return ea / jnp.sum(ea, axis=0, keepdims=True)

    def first_argmax(a, amax, n):
        rows = lax.broadcasted_iota(jnp.int32, a.shape, 0)
        return jnp.min(jnp.where(a == amax, rows, n), axis=0, keepdims=True)

    p_coarse = softmax0(logits[0:N_GROUPS])
    g_prob = jnp.max(p_coarse, axis=0, keepdims=True)
    g_idx = first_argmax(p_coarse, g_prob, N_GROUPS)
    fine = jnp.zeros((EXPERTS_PER_GROUP, tm), F32)
    for g in range(N_GROUPS):
        r0 = ROUTER_FINE_ROW0 + g * EXPERTS_PER_GROUP
        fine = jnp.where(g_idx == g, logits[r0:r0 + EXPERTS_PER_GROUP], fine)
    p_fine = softmax0(fine)
    p1 = jnp.max(p_fine, axis=0, keepdims=True)
    i1 = first_argmax(p_fine, p1, EXPERTS_PER_GROUP)
    rows = lax.broadcasted_iota(jnp.int32, p_fine.shape, 0)
    rest = jnp.where(rows == i1, -1.0, p_fine)
    p2 = jnp.max(rest, axis=0, keepdims=True)
    i2 = first_argmax(rest, p2, EXPERTS_PER_GROUP)
    denom = p1 + p2
    out_rows = lax.broadcasted_iota(jnp.int32, eid_ref.shape, 0)
    e1 = g_idx * EXPERTS_PER_GROUP + i1
    e2 = g_idx * EXPERTS_PER_GROUP + i2
    eid_ref[...] = jnp.where(out_rows == 0, e1, jnp.where(out_rows == 1, e2, 0))
    gate_ref[...] = jnp.where(out_rows == 0, g_prob * p1 / denom,
                              jnp.where(out_rows == 1, g_prob * p2 / denom, 0.0))


def _route(h, gain, w_hi, w_lo, bias):
    t, d = h.shape
    tm = _tile(t, NORM_TM)
    r = w_hi.shape[0]
    out_rows = 8
    return pl.pallas_call(
        _route_kernel,
        grid=(t // tm,),
        in_specs=[pl.BlockSpec((tm, d), lambda i: (i, 0)),
                  pl.BlockSpec((1, d), lambda i: (0, 0)),
                  pl.BlockSpec((r, d), lambda i: (0, 0)),
                  pl.BlockSpec((r, d), lambda i: (0, 0)),
                  pl.BlockSpec((r, 1), lambda i: (0, 0))],
        out_specs=[pl.BlockSpec((out_rows, tm), lambda i: (0, i)),
                   pl.BlockSpec((out_rows, tm), lambda i: (0, i))],
        out_shape=[jax.ShapeDtypeStruct((out_rows, t), jnp.int32),
                   jax.ShapeDtypeStruct((out_rows, t), F32)],
        compiler_params=_params(("arbitrary",), 40),
        name="route",
    )(h, gain.reshape(1, d), w_hi, w_lo, bias)


def _row_copy(src_hbm, row, dst, slot, sem):
    return pltpu.make_async_copy(src_hbm.at[pl.ds(row, 1)], dst.at[pl.ds(slot, 1)], sem)


def _gather_kernel(tok_ref, h_hbm, o_ref, sem):
    tm = o_ref.shape[0]

    def start(pair, carry):
        for queue in range(DMA_QUEUES):
            r = pair * DMA_QUEUES + queue
            _row_copy(h_hbm, tok_ref[0, 0, r], o_ref, r, sem).start(priority=queue)
        return carry

    def wait(r, carry):
        _row_copy(h_hbm, 0, o_ref, r, sem).wait()
        return carry

    lax.fori_loop(0, tm // DMA_QUEUES, start, 0)
    lax.fori_loop(0, tm, wait, 0)


def _gather(h, row_tok):
    n_blocks, _, tm = row_tok.shape
    d = h.shape[1]
    return pl.pallas_call(
        _gather_kernel,
        grid=(n_blocks,),
        in_specs=[pl.BlockSpec((1, 1, tm), lambda i: (i, 0, 0), memory_space=pltpu.SMEM),
                  pl.BlockSpec(memory_space=pl.ANY)],
        out_specs=pl.BlockSpec((tm, d), lambda i: (i, 0)),
        out_shape=jax.ShapeDtypeStruct((n_blocks * tm, d), F32),
        scratch_shapes=[pltpu.SemaphoreType.DMA(())],
        compiler_params=_params(("arbitrary",), 32),
        name="moe_gather",
    )(row_tok, h)


def _experts_kernel(be_ref, nu_ref, x_ref, g_ref, wg_ref, wu_ref, wd_ref, o_ref):
    @pl.when(pl.program_id(0) < nu_ref[0])
    def _run():
        half = o_ref.shape[1]
        x = _rms(x_ref[...], g_ref[...]).astype(BF16)
        a = jnp.dot(x, wg_ref[0], preferred_element_type=F32)
        b = jnp.dot(x, wu_ref[0], preferred_element_type=F32)
        hid = (a / (1.0 + jnp.exp(-a)) * b).astype(BF16)
        y = jnp.dot(hid, wd_ref[0], preferred_element_type=F32)
        hi = lax.bitcast_convert_type(y[:, :half].astype(BF16).astype(F32), jnp.uint32)
        lo = lax.bitcast_convert_type(y[:, half:].astype(BF16).astype(F32), jnp.uint32)
        o_ref[...] = (hi & jnp.uint32(0xFFFF0000)) | (lo >> 16)

    @pl.when(pl.program_id(0) >= nu_ref[0])
    def _unused_tail():
        o_ref[...] = jnp.zeros_like(o_ref)


def _experts(xs, gain, block_e, n_used, wg, wu, wd, tm):
    n_rows, d = xs.shape
    f = wg.shape[2]
    n_blocks = n_rows // tm

    def row_map(i, be, nu):
        return (jnp.minimum(i, nu[0] - 1), 0)

    def w_map(i, be, nu):
        return (be[jnp.minimum(i, nu[0] - 1)], 0, 0)

    return pl.pallas_call(
        _experts_kernel,
        grid_spec=pltpu.PrefetchScalarGridSpec(
            num_scalar_prefetch=2,
            grid=(n_blocks,),
            in_specs=[pl.BlockSpec((tm, d), row_map),
                      pl.BlockSpec((1, d), lambda i, be, nu: (0, 0)),
                      pl.BlockSpec((1, d, f), w_map),
                      pl.BlockSpec((1, d, f), w_map),
                      pl.BlockSpec((1, f, d), w_map)],
            out_specs=pl.BlockSpec((tm, d // 2), lambda i, be, nu: (i, 0))),
        out_shape=jax.ShapeDtypeStruct((n_rows, d // 2), jnp.uint32),
        compiler_params=_params(("arbitrary",), 48),
        name="moe_experts",
    )(block_e, n_used, xs, gain.reshape(1, d), wg, wu, wd)


def _combine_kernel(d_ref, g_ref, h_ref, ys_hbm, o_ref, buf, sem):
    tc, half = buf.shape[1], buf.shape[2]

    def start(r, carry):
        for k in range(TOP_K):
            _row_copy(ys_hbm, d_ref[0, 0, k * tc + r], buf.at[k], r, sem).start(priority=k % DMA_QUEUES)
        return carry

    def wait(r, carry):
        for k in range(TOP_K):
            _row_copy(ys_hbm, 0, buf.at[k], r, sem).wait()
        return carry

    lax.fori_loop(0, tc, start, 0)
    lax.fori_loop(0, tc, wait, 0)
    g = g_ref[...]
    y_hi = jnp.zeros((tc, half), F32)
    y_lo = jnp.zeros((tc, half), F32)
    for k in range(TOP_K):
        w = buf[k]
        gk = g[:, k:k + 1]
        y_hi += gk * lax.bitcast_convert_type(w & jnp.uint32(0xFFFF0000), F32)
        y_lo += gk * lax.bitcast_convert_type(w << 16, F32)
    o_ref[:, :half] = h_ref[:, :half] + y_hi
    o_ref[:, half:] = h_ref[:, half:] + y_lo


def _combine(h, ys, dest, gates):
    t, d = h.shape
    n_tb, _, two_tc = dest.shape
    tc = two_tc // TOP_K
    return pl.pallas_call(
        _combine_kernel,
        grid=(n_tb,),
        in_specs=[pl.BlockSpec((1, 1, two_tc), lambda i: (i, 0, 0), memory_space=pltpu.SMEM),
                  pl.BlockSpec((tc, TOP_K), lambda i: (i, 0)),
                  pl.BlockSpec((tc, d), lambda i: (i, 0)),
                  pl.BlockSpec(memory_space=pl.ANY)],
        out_specs=pl.BlockSpec((tc, d), lambda i: (i, 0)),
        out_shape=jax.ShapeDtypeStruct((t, d), F32),
        scratch_shapes=[pltpu.VMEM((TOP_K, tc, d // 2), jnp.uint32), pltpu.SemaphoreType.DMA(())],
        compiler_params=_params(("arbitrary",), 40),
        name="moe_combine",
    )(dest, gates, h, ys)


def _dispatch_plan(eid, t, tm, tc):
    flat_e = eid[:TOP_K].reshape(-1)
    onehot = (flat_e[:, None] == jnp.arange(N_EXPERTS, dtype=jnp.int32)[None, :]).astype(jnp.int32)
    csum = jnp.cumsum(onehot, axis=0)
    counts = csum[-1]
    rank = jnp.sum(onehot * csum, axis=1) - 1
    padded = (counts + tm - 1) // tm * tm
    pends = jnp.cumsum(padded)
    pstarts = pends - padded
    dest = jnp.sum(onehot * pstarts[None, :], axis=1) + rank
    n_blocks = -(-(TOP_K * t + N_EXPERTS * (tm - 1)) // tm)
    flat_tok = jnp.tile(jnp.arange(t, dtype=jnp.int32), TOP_K)
    row_tok = jnp.zeros((n_blocks * tm,), jnp.int32).at[dest].set(flat_tok, unique_indices=True)
    starts = jnp.arange(n_blocks, dtype=jnp.int32) * tm
    block_e = jnp.minimum(jnp.sum((starts[:, None] >= pends[None, :]).astype(jnp.int32), axis=1),
                          N_EXPERTS - 1).astype(jnp.int32)
    n_used = (pends[-1] // tm).astype(jnp.int32).reshape(1)
    dest_blocks = dest.reshape(TOP_K, t // tc, tc).transpose(1, 0, 2).reshape(t // tc, 1, TOP_K * tc)
    return row_tok.reshape(n_blocks, 1, tm), block_e, n_used, dest_blocks.astype(jnp.int32)


def _hi_lo_rows(w_t, rows):
    padded = jnp.zeros((rows, w_t.shape[1]), F32).at[:w_t.shape[0]].set(w_t)
    return _split_bf16(padded)


def _token_mixers(h, layer, bias_tabs, rel_bias, norm_attn, w_in, b_f, q_gain_diff, k_gain_diff, lam, subln_gain,
                  q_gain_fox, k_gain_fox, out_gain_sb, out_gain_fox, w_out, batch, seq):
    n_qkv = 3 * D_MODEL
    wf_hi, wf_lo = _hi_lo_rows(w_in[:, n_qkv:].T, GATE_ROWS)
    u, fc_t = _attn_norm(h, norm_attn, wf_hi, wf_lo)
    proj = _in_proj(u, w_in, n_qkv)

    b_pad = jnp.zeros((GATE_ROWS, 1), F32).at[:H_FOX, 0].set(b_f)
    cum_t = _gate_cumsum(fc_t, b_pad, batch)
    tq = bias_tabs.shape[-1]
    nq = seq // tq
    cum = cum_t.reshape(GATE_ROWS, batch, nq, 1, tq).transpose(1, 0, 2, 3, 4).reshape(batch * GATE_ROWS, nq, 1, tq)

    lam_init = 0.8 - 0.6 * math.exp(-0.3 * layer)
    lamf = lam.astype(F32)
    lam_val = jnp.exp(jnp.sum(lamf[0] * lamf[1])) - jnp.exp(jnp.sum(lamf[2] * lamf[3])) + lam_init

    o_sb = _sb_attention(proj, out_gain_sb, batch, seq)
    o_diff = _diff_attention(proj, lam_val, lam_init, bias_tabs, rel_bias, q_gain_diff.reshape(-1),
                             k_gain_diff.reshape(-1), subln_gain, batch, seq)
    o_fox = _fox_attention(proj, cum, q_gain_fox, k_gain_fox, out_gain_fox, batch, seq)

    return _out_proj([o_sb, o_diff, o_fox], w_out, h)


def _moe(h, norm_ffn, w_coarse, b_coarse, w_fine, b_fine, w_gate, w_up, w_down):
    t, d = h.shape
    assert EXPERTS_PER_GROUP == 8 and N_GROUPS <= ROUTER_FINE_ROW0 and TOP_K == 2
    w_router = jnp.zeros((ROUTER_ROWS, d), F32)
    w_router = w_router.at[:N_GROUPS].set(w_coarse.T)
    w_router = w_router.at[ROUTER_FINE_ROW0:ROUTER_FINE_ROW0 + N_EXPERTS].set(w_fine.T)
    b_router = jnp.zeros((ROUTER_ROWS, 1), F32)
    b_router = b_router.at[:N_GROUPS, 0].set(b_coarse)
    b_router = b_router.at[ROUTER_FINE_ROW0:ROUTER_FINE_ROW0 + N_EXPERTS, 0].set(b_fine)
    wr_hi, wr_lo = _split_bf16(w_router)
    eid, gate = _route(h, norm_ffn, wr_hi, wr_lo, b_router)

    tm = MOE_TM
    tc = _tile(t, COMBINE_TC, 8)
    row_tok, block_e, n_used, dest = _dispatch_plan(eid, t, tm, tc)
    xs = _gather(h, row_tok)
    ys = _experts(xs, norm_ffn, block_e, n_used, w_gate.astype(BF16), w_up.astype(BF16),
                  w_down.astype(BF16), tm)
    return _combine(h, ys, dest, gate[:TOP_K].T)


def kernel(x, norm_attn, w_in, b_f, q_gain_diff, k_gain_diff, lam, subln_gain, q_gain_fox, k_gain_fox,
           out_gain_sb, out_gain_fox, rel_bias, w_out, norm_ffn, w_coarse, b_coarse, w_fine, b_fine,
           w_gate, w_up, w_down):
    batch, seq, d = x.shape
    h = x.reshape(batch * seq, d)
    bias_tabs = _diff_bias_tables(rel_bias, _tile(seq, ATT_T))
    for layer in range(DEPTH):
        h = _token_mixers(h, layer, bias_tabs, rel_bias, norm_attn[layer], w_in[layer], b_f[layer],
                          q_gain_diff[layer], k_gain_diff[layer], lam[layer], subln_gain[layer],
                          q_gain_fox[layer], k_gain_fox[layer], out_gain_sb[layer], out_gain_fox[layer],
                          w_out[layer], batch, seq)
        h = _moe(h, norm_ffn[layer], w_coarse[layer], b_coarse[layer], w_fine[layer], b_fine[layer],
                 w_gate[layer], w_up[layer], w_down[layer])
    return h.reshape(batch, seq, d)
```

```python
import functools
import math

import jax
import jax.numpy as jnp
from jax import lax
from jax.experimental import pallas as pl
from jax.experimental.pallas import tpu as pltpu

D_MODEL = 4096
BATCH = 2
SEQ = 8192
DEPTH = 2
CHUNK = 64
HEAD_DIM = 128
HALF_DIM = HEAD_DIM // 2
N_HEADS = D_MODEL // HEAD_DIM
H_DIFF = (N_HEADS * 5) // 16
H_FOX = (N_HEADS * 5) // 16
H_SB = N_HEADS - H_DIFF - H_FOX
N_BUCKETS = 32
MAX_DISTANCE = 128
N_GROUPS = 4
EXPERTS_PER_GROUP = 8
N_EXPERTS = N_GROUPS * EXPERTS_PER_GROUP
TOP_K = 2
D_EXPERT = 512
EPS = 1e-6

F32 = jnp.float32
BF16 = jnp.bfloat16
NEG = -1e30
LOG2E = math.log2(math.e)
SB_UNDERFLOW = 104.0
SAFE_EXP2_RANGE = 64.0
EXP2_UNDERFLOW = 150.0
SUBLANES_BF16 = 16
LANES = 128
GATE_ROWS = 16
ROUTER_FINE_ROW0 = 8
ROUTER_ROWS = 48

NORM_TM = 512
PROJ_TM = 1024
PROJ_TN = 512
ATT_T = 512
KPREP_ROWS = 512
OUT_TM = 1024
OUT_TN = 512
MOE_TM = 256
GATHER_ROWS = 1024
DMA_QUEUES = 2
COMBINE_TC = 256

_NT = (((1,), (1,)), ((), ()))


def _params(semantics, vmem_mib):
    return pltpu.CompilerParams(dimension_semantics=semantics, vmem_limit_bytes=vmem_mib << 20)


def _tile(n, pref, unit=LANES):
    t = min(pref, n) // unit * unit
    while n % t:
        t -= unit
    return t


def _rms(x, gain):
    return x * lax.rsqrt(jnp.mean(x * x, axis=-1, keepdims=True) + EPS) * gain


def _split_bf16(a):
    hi = a.astype(BF16)
    lo = (a - hi.astype(F32)).astype(BF16)
    return hi, lo


_HIGH_HALF = 0xFFFF0000


def _pack_bf16_pairs(y):
    half = y.shape[1] // 2
    hi = lax.bitcast_convert_type(y[:, :half].astype(BF16).astype(F32), jnp.uint32)
    lo = lax.bitcast_convert_type(y[:, half:].astype(BF16).astype(F32), jnp.uint32)
    return (hi & jnp.uint32(_HIGH_HALF)) | (lo >> 16)


def _unpack_bf16_pairs(w):
    return (lax.bitcast_convert_type(w & jnp.uint32(_HIGH_HALF), F32),
            lax.bitcast_convert_type(w << 16, F32))


def _small_proj_t(w_hi, w_lo, y):
    y_hi, y_lo = _split_bf16(y)
    p = lax.dot_general(w_hi, y_hi, _NT, preferred_element_type=F32)
    p += lax.dot_general(w_lo, y_hi, _NT, preferred_element_type=F32)
    p += lax.dot_general(w_hi, y_lo, _NT, preferred_element_type=F32)
    return p, y_hi


def _attn_norm_kernel(x_ref, g_ref, wh_ref, wl_ref, u_ref, p_ref):
    y = _rms(x_ref[...], g_ref[...])
    p, y_hi = _small_proj_t(wh_ref[...], wl_ref[...], y)
    u_ref[...] = y_hi
    p_ref[...] = p


def _attn_norm(h, gain, w_hi, w_lo):
    t, d = h.shape
    tm = _tile(t, NORM_TM)
    r = w_hi.shape[0]
    return pl.pallas_call(
        _attn_norm_kernel,
        grid=(t // tm,),
        in_specs=[pl.BlockSpec((tm, d), lambda i: (i, 0)),
                  pl.BlockSpec((1, d), lambda i: (0, 0)),
                  pl.BlockSpec((r, d), lambda i: (0, 0)),
                  pl.BlockSpec((r, d), lambda i: (0, 0))],
        out_specs=[pl.BlockSpec((tm, d), lambda i: (i, 0)),
                   pl.BlockSpec((r, tm), lambda i: (0, i))],
        out_shape=[jax.ShapeDtypeStruct((t, d), BF16), jax.ShapeDtypeStruct((r, t), F32)],
        compiler_params=_params(("arbitrary",), 40),
        name="attn_norm",
    )(h, gain.reshape(1, d), w_hi, w_lo)


def _in_proj_kernel(u_ref, w_ref, o_ref, wb_ref):
    @pl.when(pl.program_id(1) == 0)
    def _cast_weights():
        wb_ref[...] = w_ref[...].astype(BF16)

    acc = jnp.dot(u_ref[...], wb_ref[...], preferred_element_type=F32)
    for s in range(o_ref.shape[0]):
        o_ref[s] = acc[:, s * HEAD_DIM:(s + 1) * HEAD_DIM].astype(BF16)


def _in_proj(u, w, layer, n):
    t, d = u.shape
    tm, tn = _tile(t, PROJ_TM), _tile(n, PROJ_TN)
    return pl.pallas_call(
        _in_proj_kernel,
        grid=(n // tn, t // tm),
        in_specs=[pl.BlockSpec((tm, d), lambda j, i: (i, 0)),
                  pl.BlockSpec((None, d, tn), lambda j, i: (layer, 0, j))],
        out_specs=pl.BlockSpec((tn // HEAD_DIM, tm, HEAD_DIM), lambda j, i: (j, i, 0)),
        out_shape=jax.ShapeDtypeStruct((n // HEAD_DIM, t, HEAD_DIM), BF16),
        scratch_shapes=[pltpu.VMEM((d, tn), BF16)],
        compiler_params=_params(("arbitrary", "arbitrary"), 48),
        name="in_proj",
    )(u, w)


def _gate_cumsum_kernel(fc_ref, b_ref, tri_ref, o_ref):
    x = fc_ref[...] + b_ref[...]
    log_f = jnp.minimum(x, 0.0) - jnp.log(1.0 + jnp.exp(-jnp.abs(x)))
    tri = tri_ref[...]
    carry = jnp.zeros((x.shape[0], 1), F32)
    for c in range(x.shape[1] // LANES):
        seg = log_f[:, c * LANES:(c + 1) * LANES]
        a = seg.astype(BF16)
        rest = seg - a.astype(F32)
        b, c3 = _split_bf16(rest)
        cs = (jnp.dot(a, tri, preferred_element_type=F32) + jnp.dot(b, tri, preferred_element_type=F32)
              + jnp.dot(c3, tri, preferred_element_type=F32)) + carry
        o_ref[:, c * LANES:(c + 1) * LANES] = cs
        carry = cs[:, LANES - 1:LANES]


def _gate_cumsum(fc_t, b_f, batch):
    r, t = fc_t.shape
    s = t // batch
    idx = jnp.arange(LANES)
    tri = (idx[:, None] <= idx[None, :]).astype(BF16)
    return pl.pallas_call(
        _gate_cumsum_kernel,
        grid=(batch,),
        in_specs=[pl.BlockSpec((r, s), lambda b: (0, b)),
                  pl.BlockSpec((r, 1), lambda b: (0, 0)),
                  pl.BlockSpec((LANES, LANES), lambda b: (0, 0))],
        out_specs=pl.BlockSpec((r, s), lambda b: (0, b)),
        out_shape=jax.ShapeDtypeStruct((r, t), F32),
        compiler_params=_params(("arbitrary",), 32),
        name="gate_cumsum",
    )(fc_t, b_f, tri)


def _kv_rows(j, tk):
    return pl.ds(pl.multiple_of(j * tk, tk), tk)


def _with_ones(v):
    return jnp.concatenate([v, jnp.ones_like(v)], axis=1)


def _row_max(e):
    part = e[:, 0:LANES]
    for c in range(1, e.shape[1] // LANES):
        part = jnp.maximum(part, e[:, c * LANES:(c + 1) * LANES])
    return jnp.max(part, axis=-1, keepdims=True)


def _softmax_first(e, vx, m_ref, acc_ref):
    m = _row_max(e)
    m_ref[...] = jnp.broadcast_to(m, m_ref.shape)
    acc_ref[...] = jnp.dot(jnp.exp2(e - m).astype(BF16), vx, preferred_element_type=F32)


def _softmax_fixed(e, vx, m_ref, acc_ref):
    m = m_ref[...]
    p = jnp.concatenate([jnp.exp2(e[:, c * LANES:(c + 1) * LANES] - m)
                         for c in range(e.shape[1] // LANES)], axis=1)
    acc_ref[...] += jnp.dot(p.astype(BF16), vx, preferred_element_type=F32)


def _softmax_online(e, vx, m_ref, acc_ref):
    m_old = m_ref[:, 0:1]
    m_new = jnp.maximum(m_old, _row_max(e))
    alpha = jnp.exp2(m_old - m_new)
    p = jnp.exp2(e - m_new).astype(BF16)
    acc_ref[...] = alpha * acc_ref[...] + jnp.dot(p, vx, preferred_element_type=F32)
    m_ref[...] = jnp.broadcast_to(m_new, m_ref.shape)


def _sweep_left(i, first_step, safe, step_fn, fixed_count=None):
    count = jnp.maximum(i - first_step + 1, 0)
    near = count if fixed_count is None else jnp.minimum(count, fixed_count)

    @pl.when(safe)
    def _fixed():
        def pair(step, carry):
            step_fn(i - first_step - 1 - 2 * step, 2, _softmax_fixed)
            return carry
        lax.fori_loop(0, lax.shift_right_logical(near, 1), pair, 0)

        @pl.when((near & 1) == 1)
        def _last():
            step_fn(i - first_step + 1 - near, 1, _softmax_fixed)

    @pl.when(jnp.logical_not(safe))
    def _online():
        def single(step, carry):
            step_fn(i - first_step - step, 1, _softmax_online)
            return carry
        lax.fori_loop(0, count, single, 0)


def _softmax_result(acc_ref):
    acc = acc_ref[...]
    return acc[:, :HEAD_DIM] / acc[:, HEAD_DIM:]


def _head_specs(t_blocks_per_batch, tq, seq, q_slot, k_slot, v_slot):
    nq = t_blocks_per_batch
    return [pl.BlockSpec((1, tq, HEAD_DIM), lambda b, h, i: (q_slot + h, b * nq + i, 0)),
            pl.BlockSpec((1, seq, HEAD_DIM), lambda b, h, i: (k_slot + h, b, 0)),
            pl.BlockSpec((1, seq, HEAD_DIM), lambda b, h, i: (v_slot + h, b, 0))]


def _sb_kernel(q_ref, k_ref, v_ref, tri_ref, g_ref, o_ref, acc_ref, run_ref):
    i = pl.program_id(2)
    tq = q_ref.shape[1]
    q = (q_ref[0].astype(F32) * (1.0 / math.sqrt(HEAD_DIM))).astype(BF16)
    row = lax.broadcasted_iota(jnp.int32, (tq, tq), 0)
    col = lax.broadcasted_iota(jnp.int32, (tq, tq), 1)
    strict = col < row
    acc_ref[...] = jnp.zeros_like(acc_ref)
    run_ref[...] = jnp.zeros_like(run_ref)

    def block(j, masked):
        k = k_ref[0, _kv_rows(j, tq), :]
        v = v_ref[0, _kv_rows(j, tq), :]
        z = lax.dot_general(q, k, _NT, preferred_element_type=F32)
        sp = jnp.maximum(z, 0.0) + jnp.log(1.0 + jnp.exp(-jnp.abs(z)))
        if masked:
            sp = jnp.where(strict, sp, 0.0)
        suffix = jnp.dot(sp.astype(BF16), tri_ref[...], preferred_element_type=F32)
        w = jnp.exp(z - suffix - run_ref[...])
        if masked:
            w = jnp.where(strict, w, 0.0)
        acc_ref[...] += jnp.dot(w.astype(BF16), v, preferred_element_type=F32)
        run_ref[...] += suffix[:, 0:1]

    block(i, True)

    def more(carry):
        step, smallest_run = carry
        return jnp.logical_and(step < i, smallest_run < SB_UNDERFLOW)

    def body(carry):
        step, _ = carry
        block(i - 1 - step, False)
        return step + 1, jnp.min(run_ref[...])

    lax.while_loop(more, body, (0, jnp.min(run_ref[...])))
    o_ref[...] = _rms(acc_ref[...], g_ref[...]).astype(BF16)


def _sb_attention(proj, gain, batch, seq):
    t = proj.shape[1]
    tq = _tile(seq, ATT_T)
    nq = seq // tq
    idx = jnp.arange(tq)
    tri = (idx[:, None] >= idx[None, :]).astype(BF16)
    return pl.pallas_call(
        _sb_kernel,
        grid=(batch, H_SB, nq),
        in_specs=_head_specs(nq, tq, seq, 0, H_SB, 2 * H_SB) + [
            pl.BlockSpec((tq, tq), lambda b, h, i: (0, 0)),
            pl.BlockSpec((1, HEAD_DIM), lambda b, h, i: (0, 0))],
        out_specs=pl.BlockSpec((tq, HEAD_DIM), lambda b, h, i: (b * nq + i, h)),
        out_shape=jax.ShapeDtypeStruct((t, H_SB * HEAD_DIM), BF16),
        scratch_shapes=[pltpu.VMEM((tq, HEAD_DIM), F32), pltpu.VMEM((tq, 1), F32)],
        compiler_params=_params(("arbitrary", "arbitrary", "arbitrary"), 40),
        name="sb_attention",
    )(proj, proj, proj, tri, gain.reshape(1, HEAD_DIM))


def _sub_rms(x, gain):
    lo = lax.broadcasted_iota(jnp.int32, x.shape, 1) < HALF_DIM
    sq = x * x
    ms_lo = jnp.sum(jnp.where(lo, sq, 0.0), axis=-1, keepdims=True) * (1.0 / HALF_DIM)
    ms_hi = jnp.sum(jnp.where(lo, 0.0, sq), axis=-1, keepdims=True) * (1.0 / HALF_DIM)
    inv = jnp.where(lo, lax.rsqrt(ms_lo + EPS), lax.rsqrt(ms_hi + EPS))
    return x * inv * gain


def _diff_kernel(lam_init, lam_ref, safe_ref, q_ref, k_ref, v_ref, bias_ref, gq_ref, gk_ref, go_ref, o_ref,
                 kp_ref, acc_ref, m_ref):
    i = pl.program_id(2)
    tq = q_ref.shape[1]
    seq = k_ref.shape[1]

    @pl.when(i == 0)
    def _prep_keys():
        def prep(c, carry):
            rows = _kv_rows(c, KPREP_ROWS)
            kn = _sub_rms(k_ref[0, rows, :].astype(F32), gk_ref[...])
            lo = lax.broadcasted_iota(jnp.int32, kn.shape, 1) < HALF_DIM
            kp_ref[0, rows, :] = jnp.where(lo, kn, 0.0).astype(BF16)
            kp_ref[1, rows, :] = jnp.where(lo, 0.0, kn).astype(BF16)
            return carry
        lax.fori_loop(0, seq // KPREP_ROWS, prep, 0)

    q = (_sub_rms(q_ref[0].astype(F32), gq_ref[...]) * (LOG2E / math.sqrt(HALF_DIM))).astype(BF16)
    safe = safe_ref[pl.program_id(1)] != 0

    def block(j, n, table, update):
        rows = pl.ds(pl.multiple_of(j * tq, tq), n * tq)
        vx = _with_ones(v_ref[0, rows, :])
        for sub in range(2):
            e = lax.dot_general(q, kp_ref[sub, rows, :], _NT, preferred_element_type=F32)
            if table is not None:
                e = e + bias_ref[0, table]
            update(e, vx, m_ref.at[sub], acc_ref.at[sub])

    block(i, 1, 0, _softmax_first)
    for flag, update in ((safe, _softmax_fixed), (jnp.logical_not(safe), _softmax_online)):
        @pl.when(jnp.logical_and(i >= 1, flag))
        def _near(update=update):
            block(i - 1, 1, 1, update)
    _sweep_left(i, 2, safe, lambda j, n, update: block(j, n, None, update))
    o = _softmax_result(acc_ref.at[0]) - lam_ref[0] * _softmax_result(acc_ref.at[1])
    o_ref[...] = (_rms(o, go_ref[...]) * (1.0 - lam_init)).astype(BF16)


def _relative_bucket(rel):
    half = N_BUCKETS // 2
    max_exact = half // 2
    n = jnp.abs(rel)
    large = max_exact + (jnp.log(jnp.maximum(n, 1).astype(F32) / max_exact)
                         / math.log(MAX_DISTANCE / max_exact) * (half - max_exact)).astype(jnp.int32)
    large = jnp.minimum(large, half - 1)
    return jnp.where(rel > 0, half, 0) + jnp.where(n < max_exact, n, large)


def _diff_bias_tables(rel_bias, tq):
    assert tq >= MAX_DISTANCE and tq % CHUNK == 0
    t = jnp.arange(tq)[:, None]
    s = jnp.arange(tq)[None, :]
    rel = jnp.stack([s - t - delta * tq for delta in range(2)])
    bucket = _relative_bucket(rel)[None]
    shifted = ((rel_bias - rel_bias[N_BUCKETS // 2 - 1]) * LOG2E).astype(F32)
    tab = jnp.zeros((rel_bias.shape[1], 2, tq, tq), F32)
    for b in range(N_BUCKETS):
        tab = jnp.where(bucket == b, shifted[b][:, None, None, None], tab)
    visible = jnp.stack([(s // CHUNK) <= (t // CHUNK), jnp.ones((tq, tq), bool)])
    return jnp.where(visible[None], tab, NEG)


def _diff_attention(proj, lam_val, lam_init, bias_tabs, rel_bias, gq, gk, go, batch, seq):
    t = proj.shape[1]
    tq = bias_tabs.shape[-1]
    nq = seq // tq
    base = 3 * H_SB
    vec = pl.BlockSpec((1, HEAD_DIM), lambda b, h, i: (0, 0))
    bound = 2.0 * math.sqrt(HALF_DIM) * jnp.max(jnp.abs(gq)) * jnp.max(jnp.abs(gk))
    spread = jnp.max(rel_bias, axis=0) - jnp.min(rel_bias, axis=0)
    safe = ((bound + spread) * LOG2E < SAFE_EXP2_RANGE).astype(jnp.int32)
    return pl.pallas_call(
        functools.partial(_diff_kernel, lam_init),
        grid=(batch, H_DIFF, nq),
        in_specs=[pl.BlockSpec(memory_space=pltpu.SMEM), pl.BlockSpec(memory_space=pltpu.SMEM)]
        + _head_specs(nq, tq, seq, base, base + H_DIFF, base + 2 * H_DIFF)
        + [pl.BlockSpec((1, 2, tq, tq), lambda b, h, i: (h, 0, 0, 0)), vec, vec, vec],
        out_specs=pl.BlockSpec((tq, HEAD_DIM), lambda b, h, i: (b * nq + i, h)),
        out_shape=jax.ShapeDtypeStruct((t, H_DIFF * HEAD_DIM), BF16),
        scratch_shapes=[pltpu.VMEM((2, seq, HEAD_DIM), BF16), pltpu.VMEM((2, tq, 2 * HEAD_DIM), F32),
                        pltpu.VMEM((2, tq, LANES), F32)],
        compiler_params=_params(("arbitrary", "arbitrary", "arbitrary"), 48),
        name="diff_attention",
    )(lam_val.reshape(1), safe, proj, proj, proj, bias_tabs, gq.reshape(1, HEAD_DIM), gk.reshape(1, HEAD_DIM),
      go.reshape(1, HEAD_DIM))


def _fox_kernel(safe_ref, margin_ref, edge_ref, q_ref, k_ref, v_ref, c_ref, gq_ref, gk_ref, go_ref, o_ref,
                kn_ref, acc_ref, m_ref):
    i = pl.program_id(2)
    tq = q_ref.shape[1]
    seq = k_ref.shape[1]
    nq = seq // tq
    edges = (pl.program_id(0) * GATE_ROWS + pl.program_id(1)) * (2 * nq)
    reach = edge_ref[edges + i] + margin_ref[0]
    live_blocks = jnp.int32(0)
    for j in range(nq):
        live_blocks += jnp.logical_and(j < i, edge_ref[edges + nq + j] <= reach).astype(jnp.int32)

    @pl.when(i == 0)
    def _prep_keys():
        def prep(c, carry):
            rows = _kv_rows(c, KPREP_ROWS)
            kn_ref[rows, :] = _rms(k_ref[0, rows, :].astype(F32), gk_ref[...]).astype(BF16)
            return carry
        lax.fori_loop(0, seq // KPREP_ROWS, prep, 0)

    q = (_rms(q_ref[0].astype(F32), gq_ref[...]) * (LOG2E / math.sqrt(HEAD_DIM))).astype(BF16)
    row = lax.broadcasted_iota(jnp.int32, (tq, tq), 0)
    col = lax.broadcasted_iota(jnp.int32, (tq, tq), 1)
    causal = col <= row
    c_first = c_ref[0, i][:, 0:1]

    def block(j, n, masked, update):
        rows = pl.ds(pl.multiple_of(j * tq, tq), n * tq)
        e = lax.dot_general(q, kn_ref[rows, :], _NT, preferred_element_type=F32)
        c_keys = jnp.concatenate([c_ref[0, j + b] for b in range(n)], axis=1)
        e = e + (c_first - c_keys) * LOG2E
        if masked:
            e = jnp.where(causal, e, NEG)
        update(e, _with_ones(v_ref[0, rows, :]), m_ref, acc_ref)

    block(i, 1, True, _softmax_first)
    _sweep_left(i, 1, safe_ref[0] != 0, lambda j, n, update: block(j, n, False, update), live_blocks)
    o_ref[...] = _rms(_softmax_result(acc_ref), go_ref[...]).astype(BF16)


def _fox_attention(proj, cum, gq, gk, go, batch, seq):
    t = proj.shape[1]
    nq, tq = cum.shape[1], cum.shape[3]
    base = 3 * (H_SB + H_DIFF)
    vec = pl.BlockSpec((1, HEAD_DIM), lambda b, h, i: (0, 0))
    bound = 2.0 * math.sqrt(HEAD_DIM) * jnp.max(jnp.abs(gq)) * jnp.max(jnp.abs(gk))
    safe = (bound * LOG2E < SAFE_EXP2_RANGE).astype(jnp.int32).reshape(1)
    margin = (bound + EXP2_UNDERFLOW / LOG2E).astype(F32).reshape(1)
    edges = jnp.concatenate([cum[:, :, 0, 0], cum[:, :, 0, tq - 1]], axis=1).reshape(-1)
    smem = pl.BlockSpec(memory_space=pltpu.SMEM)
    return pl.pallas_call(
        _fox_kernel,
        grid=(batch, H_FOX, nq),
        in_specs=[smem, smem, smem]
        + _head_specs(nq, tq, seq, base, base + H_FOX, base + 2 * H_FOX)
        + [pl.BlockSpec((1, nq, 1, tq), lambda b, h, i: (b * GATE_ROWS + h, 0, 0, 0)), vec, vec, vec],
        out_specs=pl.BlockSpec((tq, HEAD_DIM), lambda b, h, i: (b * nq + i, h)),
        out_shape=jax.ShapeDtypeStruct((t, H_FOX * HEAD_DIM), BF16),
        scratch_shapes=[pltpu.VMEM((seq, HEAD_DIM), BF16), pltpu.VMEM((tq, 2 * HEAD_DIM), F32),
                        pltpu.VMEM((tq, LANES), F32)],
        compiler_params=_params(("arbitrary", "arbitrary", "arbitrary"), 40),
        name="fox_attention",
    )(safe, margin, edges, proj, proj, proj, cum, gq.reshape(1, HEAD_DIM), gk.reshape(1, HEAD_DIM),
      go.reshape(1, HEAD_DIM))


def _out_proj_kernel(a_sb, a_diff, a_fox, w_ref, h_ref, o_ref, wb_ref):
    @pl.when(pl.program_id(1) == 0)
    def _cast_weights():
        wb_ref[...] = w_ref[...].astype(BF16)

    acc = h_ref[...]
    row = 0
    for a in (a_sb, a_diff, a_fox):
        width = a.shape[1]
        acc += jnp.dot(a[...], wb_ref[row:row + width, :], preferred_element_type=F32)
        row += width
    o_ref[...] = acc


def _out_proj(heads, w, layer, h):
    t, d = h.shape
    tm, tn = _tile(t, OUT_TM), _tile(d, OUT_TN)
    return pl.pallas_call(
        _out_proj_kernel,
        grid=(d // tn, t // tm),
        in_specs=[pl.BlockSpec((tm, a.shape[1]), lambda j, i: (i, 0)) for a in heads]
        + [pl.BlockSpec((None, d, tn), lambda j, i: (layer, 0, j)),
           pl.BlockSpec((tm, tn), lambda j, i: (i, j))],
        out_specs=pl.BlockSpec((tm, tn), lambda j, i: (i, j)),
        out_shape=jax.ShapeDtypeStruct((t, d), F32),
        scratch_shapes=[pltpu.VMEM((d, tn), BF16)],
        compiler_params=_params(("arbitrary", "arbitrary"), 52),
        name="out_proj",
    )(*heads, w, h)


def _route_kernel(x_ref, g_ref, wh_ref, wl_ref, b_ref, eid_ref, gate_ref, u_ref):
    y = _rms(x_ref[...], g_ref[...])
    u_ref[...] = _pack_bf16_pairs(y)
    p, _ = _small_proj_t(wh_ref[...], wl_ref[...], y)
    logits = p + b_ref[...]
    tm = logits.shape[1]

    def softmax0(a):
        ea = jnp.exp(a - jnp.max(a, axis=0, keepdims=True))
        return ea / jnp.sum(ea, axis=0, keepdims=True)

    def first_argmax(a, amax, n):
        rows = lax.broadcasted_iota(jnp.int32, a.shape, 0)
        return jnp.min(jnp.where(a == amax, rows, n), axis=0, keepdims=True)

    p_coarse = softmax0(logits[0:N_GROUPS])
    g_prob = jnp.max(p_coarse, axis=0, keepdims=True)
    g_idx = first_argmax(p_coarse, g_prob, N_GROUPS)
    fine = jnp.zeros((EXPERTS_PER_GROUP, tm), F32)
    for g in range(N_GROUPS):
        r0 = ROUTER_FINE_ROW0 + g * EXPERTS_PER_GROUP
        fine = jnp.where(g_idx == g, logits[r0:r0 + EXPERTS_PER_GROUP], fine)
    p_fine = softmax0(fine)
    p1 = jnp.max(p_fine, axis=0, keepdims=True)
    i1 = first_argmax(p_fine, p1, EXPERTS_PER_GROUP)
    rows = lax.broadcasted_iota(jnp.int32, p_fine.shape, 0)
    rest = jnp.where(rows == i1, -1.0, p_fine)
    p2 = jnp.max(rest, axis=0, keepdims=True)
    i2 = first_argmax(rest, p2, EXPERTS_PER_GROUP)
    denom = p1 + p2
    out_rows = lax.broadcasted_iota(jnp.int32, eid_ref.shape, 0)
    e1 = g_idx * EXPERTS_PER_GROUP + i1
    e2 = g_idx * EXPERTS_PER_GROUP + i2
    eid_ref[...] = jnp.where(out_rows == 0, e1, jnp.where(out_rows == 1, e2, 0))
    gate_ref[...] = jnp.where(out_rows == 0, g_prob * p1 / denom,
                              jnp.where(out_rows == 1, g_prob * p2 / denom, 0.0))


def _route(h, gain, w_hi, w_lo, bias):
    t, d = h.shape
    tm = _tile(t, NORM_TM)
    r = w_hi.shape[0]
    out_rows = 8
    return pl.pallas_call(
        _route_kernel,
        grid=(t // tm,),
        in_specs=[pl.BlockSpec((tm, d), lambda i: (i, 0)),
                  pl.BlockSpec((1, d), lambda i: (0, 0)),
                  pl.BlockSpec((r, d), lambda i: (0, 0)),
                  pl.BlockSpec((r, d), lambda i: (0, 0)),
                  pl.BlockSpec((r, 1), lambda i: (0, 0))],
        out_specs=[pl.BlockSpec((out_rows, tm), lambda i: (0, i)),
                   pl.BlockSpec((out_rows, tm), lambda i: (0, i)),
                   pl.BlockSpec((tm, d // 2), lambda i: (i, 0))],
        out_shape=[jax.ShapeDtypeStruct((out_rows, t), jnp.int32),
                   jax.ShapeDtypeStruct((out_rows, t), F32),
                   jax.ShapeDtypeStruct((t, d // 2), jnp.uint32)],
        compiler_params=_params(("arbitrary",), 40),
        name="route",
    )(h, gain.reshape(1, d), w_hi, w_lo, bias)


def _row_copy(src_hbm, row, dst, slot, sem):
    return pltpu.make_async_copy(src_hbm.at[pl.ds(row, 1)], dst.at[pl.ds(slot, 1)], sem)


def _gather_kernel(tok_ref, h_hbm, o_ref, sem):
    tm = o_ref.shape[0]

    def start(pair, carry):
        for queue in range(DMA_QUEUES):
            r = pair * DMA_QUEUES + queue
            _row_copy(h_hbm, tok_ref[0, 0, r], o_ref, r, sem).start(priority=queue)
        return carry

    def wait(r, carry):
        _row_copy(h_hbm, 0, o_ref, r, sem).wait()
        return carry

    lax.fori_loop(0, tm // DMA_QUEUES, start, 0)
    lax.fori_loop(0, tm, wait, 0)


def _gather(u, row_tok):
    n_steps, _, rows = row_tok.shape
    width = u.shape[1]
    return pl.pallas_call(
        _gather_kernel,
        grid=(n_steps,),
        in_specs=[pl.BlockSpec((1, 1, rows), lambda i: (i, 0, 0), memory_space=pltpu.SMEM),
                  pl.BlockSpec(memory_space=pl.ANY)],
        out_specs=pl.BlockSpec((rows, width), lambda i: (i, 0)),
        out_shape=jax.ShapeDtypeStruct((n_steps * rows, width), u.dtype),
        scratch_shapes=[pltpu.SemaphoreType.DMA(())],
        compiler_params=_params(("arbitrary",), 32),
        name="moe_gather",
    )(row_tok, u)


def _experts_kernel(be_ref, nu_ref, x_ref, wg_ref, wu_ref, wd_ref, o_ref):
    @pl.when(pl.program_id(0) < nu_ref[0])
    def _run():
        x = jnp.concatenate(_unpack_bf16_pairs(x_ref[...]), axis=1).astype(BF16)
        a = jnp.dot(x, wg_ref[...], preferred_element_type=F32)
        b = jnp.dot(x, wu_ref[...], preferred_element_type=F32)
        hid = (a / (1.0 + jnp.exp(-a)) * b).astype(BF16)
        o_ref[...] = _pack_bf16_pairs(jnp.dot(hid, wd_ref[...], preferred_element_type=F32))

    @pl.when(pl.program_id(0) >= nu_ref[0])
    def _unused_tail():
        o_ref[...] = jnp.zeros_like(o_ref)


def _experts(xs, block_e, n_used, wg, wu, wd, layer, tm):
    n_rows, half = xs.shape
    d, f = wg.shape[1], wg.shape[2]
    n_blocks = n_rows // tm

    def row_map(i, be, nu):
        return (jnp.minimum(i, nu[0] - 1), 0)

    def w_map(i, be, nu):
        return (layer * N_EXPERTS + be[jnp.minimum(i, nu[0] - 1)], 0, 0)

    return pl.pallas_call(
        _experts_kernel,
        grid_spec=pltpu.PrefetchScalarGridSpec(
            num_scalar_prefetch=2,
            grid=(n_blocks,),
            in_specs=[pl.BlockSpec((tm, half), row_map),
                      pl.BlockSpec((None, d, f), w_map),
                      pl.BlockSpec((None, d, f), w_map),
                      pl.BlockSpec((None, f, d), w_map)],
            out_specs=pl.BlockSpec((tm, half), lambda i, be, nu: (i, 0))),
        out_shape=jax.ShapeDtypeStruct((n_rows, half), jnp.uint32),
        compiler_params=_params(("arbitrary",), 48),
        name="moe_experts",
    )(block_e, n_used, xs, wg, wu, wd)


def _combine_kernel(d_ref, g_ref, h_ref, ys_hbm, o_ref, buf, sem):
    tc, half = buf.shape[1], buf.shape[2]

    def start(r, carry):
        for k in range(TOP_K):
            _row_copy(ys_hbm, d_ref[0, 0, k * tc + r], buf.at[k], r, sem).start(priority=k % DMA_QUEUES)
        return carry

    def wait(r, carry):
        for k in range(TOP_K):
            _row_copy(ys_hbm, 0, buf.at[k], r, sem).wait()
        return carry

    lax.fori_loop(0, tc, start, 0)
    lax.fori_loop(0, tc, wait, 0)
    g = g_ref[...]
    y_hi = h_ref[:, :half]
    y_lo = h_ref[:, half:]
    for k in range(TOP_K):
        hi, lo = _unpack_bf16_pairs(buf[k])
        y_hi += g[:, k:k + 1] * hi
        y_lo += g[:, k:k + 1] * lo
    o_ref[:, :half] = y_hi
    o_ref[:, half:] = y_lo


def _combine(h, ys, dest, gates):
    t, d = h.shape
    n_tb, _, two_tc = dest.shape
    tc = two_tc // TOP_K
    return pl.pallas_call(
        _combine_kernel,
        grid=(n_tb,),
        in_specs=[pl.BlockSpec((1, 1, two_tc), lambda i: (i, 0, 0), memory_space=pltpu.SMEM),
                  pl.BlockSpec((tc, TOP_K), lambda i: (i, 0)),
                  pl.BlockSpec((tc, d), lambda i: (i, 0)),
                  pl.BlockSpec(memory_space=pl.ANY)],
        out_specs=pl.BlockSpec((tc, d), lambda i: (i, 0)),
        out_shape=jax.ShapeDtypeStruct((t, d), F32),
        scratch_shapes=[pltpu.VMEM((TOP_K, tc, d // 2), jnp.uint32), pltpu.SemaphoreType.DMA(())],
        compiler_params=_params(("arbitrary",), 40),
        name="moe_combine",
    )(dest, gates, h, ys)


def _dispatch_plan(eid, t, tm, tc):
    flat_e = eid[:TOP_K].reshape(-1)
    onehot = (flat_e[:, None] == jnp.arange(N_EXPERTS, dtype=jnp.int32)[None, :]).astype(jnp.int32)
    csum = jnp.cumsum(onehot, axis=0)
    counts = csum[-1]
    rank = jnp.sum(onehot * csum, axis=1) - 1
    padded = (counts + tm - 1) // tm * tm
    pends = jnp.cumsum(padded)
    pstarts = pends - padded
    dest = jnp.sum(onehot * pstarts[None, :], axis=1) + rank
    n_blocks = -(-(TOP_K * t + N_EXPERTS * (tm - 1)) // tm)
    flat_tok = jnp.tile(jnp.arange(t, dtype=jnp.int32), TOP_K)
    row_tok = jnp.zeros((n_blocks * tm,), jnp.int32).at[dest].set(flat_tok, unique_indices=True)
    starts = jnp.arange(n_blocks, dtype=jnp.int32) * tm
    block_e = jnp.minimum(jnp.sum((starts[:, None] >= pends[None, :]).astype(jnp.int32), axis=1),
                          N_EXPERTS - 1).astype(jnp.int32)
    n_used = (pends[-1] // tm).astype(jnp.int32).reshape(1)
    dest_blocks = dest.reshape(TOP_K, t // tc, tc).transpose(1, 0, 2).reshape(t // tc, 1, TOP_K * tc)
    gather_rows = _tile(n_blocks * tm, GATHER_ROWS, tm)
    return row_tok.reshape(-1, 1, gather_rows), block_e, n_used, dest_blocks.astype(jnp.int32)


def _hi_lo_rows(w_t, rows):
    padded = jnp.zeros((rows, w_t.shape[1]), F32).at[:w_t.shape[0]].set(w_t)
    return _split_bf16(padded)


def _token_mixers(h, layer, bias_tabs, rel_bias, norm_attn, w_in, b_f, q_gain_diff, k_gain_diff, lam, subln_gain,
                  q_gain_fox, k_gain_fox, out_gain_sb, out_gain_fox, w_out, batch, seq):
    n_qkv = 3 * D_MODEL
    wf_hi, wf_lo = _hi_lo_rows(w_in[layer, :, n_qkv:].T, GATE_ROWS)
    u, fc_t = _attn_norm(h, norm_attn, wf_hi, wf_lo)
    proj = _in_proj(u, w_in, layer, n_qkv)

    b_pad = jnp.zeros((GATE_ROWS, 1), F32).at[:H_FOX, 0].set(b_f)
    cum_t = _gate_cumsum(fc_t, b_pad, batch)
    tq = bias_tabs.shape[-1]
    nq = seq // tq
    cum = cum_t.reshape(GATE_ROWS, batch, nq, 1, tq).transpose(1, 0, 2, 3, 4).reshape(batch * GATE_ROWS, nq, 1, tq)

    lam_init = 0.8 - 0.6 * math.exp(-0.3 * layer)
    lamf = lam.astype(F32)
    lam_val = jnp.exp(jnp.sum(lamf[0] * lamf[1])) - jnp.exp(jnp.sum(lamf[2] * lamf[3])) + lam_init

    o_sb = _sb_attention(proj, out_gain_sb, batch, seq)
    o_diff = _diff_attention(proj, lam_val, lam_init, bias_tabs, rel_bias, q_gain_diff.reshape(-1),
                             k_gain_diff.reshape(-1), subln_gain, batch, seq)
    o_fox = _fox_attention(proj, cum, q_gain_fox, k_gain_fox, out_gain_fox, batch, seq)

    return _out_proj([o_sb, o_diff, o_fox], w_out, layer, h)


def _moe(h, layer, norm_ffn, w_coarse, b_coarse, w_fine, b_fine, w_gate, w_up, w_down):
    t, d = h.shape
    assert EXPERTS_PER_GROUP == 8 and N_GROUPS <= ROUTER_FINE_ROW0 and TOP_K == 2
    w_router = jnp.zeros((ROUTER_ROWS, d), F32)
    w_router = w_router.at[:N_GROUPS].set(w_coarse.T)
    w_router = w_router.at[ROUTER_FINE_ROW0:ROUTER_FINE_ROW0 + N_EXPERTS].set(w_fine.T)
    b_router = jnp.zeros((ROUTER_ROWS, 1), F32)
    b_router = b_router.at[:N_GROUPS, 0].set(b_coarse)
    b_router = b_router.at[ROUTER_FINE_ROW0:ROUTER_FINE_ROW0 + N_EXPERTS, 0].set(b_fine)
    wr_hi, wr_lo = _split_bf16(w_router)
    eid, gate, u = _route(h, norm_ffn, wr_hi, wr_lo, b_router)

    tm = MOE_TM
    tc = _tile(t, COMBINE_TC, 8)
    row_tok, block_e, n_used, dest = _dispatch_plan(eid, t, tm, tc)
    xs = _gather(u, row_tok)
    ys = _experts(xs, block_e, n_used, w_gate, w_up, w_down, layer, tm)
    return _combine(h, ys, dest, gate[:TOP_K].T)


def kernel(x, norm_attn, w_in, b_f, q_gain_diff, k_gain_diff, lam, subln_gain, q_gain_fox, k_gain_fox,
           out_gain_sb, out_gain_fox, rel_bias, w_out, norm_ffn, w_coarse, b_coarse, w_fine, b_fine,
           w_gate, w_up, w_down):
    batch, seq, d = x.shape
    h = x.reshape(batch * seq, d)
    bias_tabs = _diff_bias_tables(rel_bias, _tile(seq, ATT_T))
    experts = [w.astype(BF16).reshape((-1,) + w.shape[2:]) for w in (w_gate, w_up, w_down)]
    for layer in range(DEPTH):
        h = _token_mixers(h, layer, bias_tabs, rel_bias, norm_attn[layer], w_in, b_f[layer],
                          q_gain_diff[layer], k_gain_diff[layer], lam[layer], subln_gain[layer],
                          q_gain_fox[layer], k_gain_fox[layer], out_gain_sb[layer], out_gain_fox[layer],
                          w_out, batch, seq)
        h = _moe(h, layer, norm_ffn[layer], w_coarse[layer], b_coarse[layer], w_fine[layer], b_fine[layer],
                 *experts)
    return h.reshape(batch, seq, d)
```

```python
import functools
import math

import jax
import jax.numpy as jnp
from jax import lax
from jax.experimental import pallas as pl
from jax.experimental.pallas import tpu as pltpu

D_MODEL = 4096
BATCH = 2
SEQ = 8192
DEPTH = 2
CHUNK = 64
HEAD_DIM = 128
HALF_DIM = HEAD_DIM // 2
N_HEADS = D_MODEL // HEAD_DIM
H_DIFF = (N_HEADS * 5) // 16
H_FOX = (N_HEADS * 5) // 16
H_SB = N_HEADS - H_DIFF - H_FOX
N_BUCKETS = 32
MAX_DISTANCE = 128
N_GROUPS = 4
EXPERTS_PER_GROUP = 8
N_EXPERTS = N_GROUPS * EXPERTS_PER_GROUP
TOP_K = 2
D_EXPERT = 512
EPS = 1e-6

F32 = jnp.float32
BF16 = jnp.bfloat16
NEG = -1e30
LOG2E = math.log2(math.e)
SB_UNDERFLOW = 104.0
SAFE_EXP2_RANGE = 64.0
EXP2_UNDERFLOW = 150.0
SUBLANES_BF16 = 16
LANES = 128
GATE_ROWS = 16
ROUTER_FINE_ROW0 = 8
ROUTER_ROWS = 48

NORM_TM = 512
PROJ_TM = 1024
PROJ_TN = 1024
ATT_T = 512
KPREP_ROWS = 512
OUT_TM = 1024
OUT_TN = 512
MOE_TM = 256
ROWS_PER_ISSUE = 8
DMA_QUEUES = 2
COMBINE_TC = 256

_NT = (((1,), (1,)), ((), ()))


def _params(semantics, vmem_mib):
    return pltpu.CompilerParams(dimension_semantics=semantics, vmem_limit_bytes=vmem_mib << 20)


def _tile(n, pref, unit=LANES):
    t = min(pref, n) // unit * unit
    while n % t:
        t -= unit
    return t


def _rms(x, gain):
    return x * lax.rsqrt(jnp.mean(x * x, axis=-1, keepdims=True) + EPS) * gain


def _split_bf16(a):
    hi = a.astype(BF16)
    lo = (a - hi.astype(F32)).astype(BF16)
    return hi, lo


_HIGH_HALF = 0xFFFF0000


def _pack_bf16_pairs(y):
    half = y.shape[1] // 2
    hi = lax.bitcast_convert_type(y[:, :half].astype(BF16).astype(F32), jnp.uint32)
    lo = lax.bitcast_convert_type(y[:, half:].astype(BF16).astype(F32), jnp.uint32)
    return (hi & jnp.uint32(_HIGH_HALF)) | (lo >> 16)


def _unpack_bf16_pairs(w):
    return (lax.bitcast_convert_type(w & jnp.uint32(_HIGH_HALF), F32),
            lax.bitcast_convert_type(w << 16, F32))


def _small_proj_t(w_hi, w_lo, y):
    y_hi, y_lo = _split_bf16(y)
    p = lax.dot_general(w_hi, y_hi, _NT, preferred_element_type=F32)
    p += lax.dot_general(w_lo, y_hi, _NT, preferred_element_type=F32)
    p += lax.dot_general(w_hi, y_lo, _NT, preferred_element_type=F32)
    return p, y_hi


def _attn_norm_kernel(x_ref, g_ref, wh_ref, wl_ref, u_ref, p_ref):
    y = _rms(x_ref[...], g_ref[...])
    p, y_hi = _small_proj_t(wh_ref[...], wl_ref[...], y)
    u_ref[...] = y_hi
    p_ref[...] = p


def _attn_norm(h, gain, w_hi, w_lo):
    t, d = h.shape
    tm = _tile(t, NORM_TM)
    r = w_hi.shape[0]
    return pl.pallas_call(
        _attn_norm_kernel,
        grid=(t // tm,),
        in_specs=[pl.BlockSpec((tm, d), lambda i: (i, 0)),
                  pl.BlockSpec((1, d), lambda i: (0, 0)),
                  pl.BlockSpec((r, d), lambda i: (0, 0)),
                  pl.BlockSpec((r, d), lambda i: (0, 0))],
        out_specs=[pl.BlockSpec((tm, d), lambda i: (i, 0)),
                   pl.BlockSpec((r, tm), lambda i: (0, i))],
        out_shape=[jax.ShapeDtypeStruct((t, d), BF16), jax.ShapeDtypeStruct((r, t), F32)],
        compiler_params=_params(("arbitrary",), 40),
        name="attn_norm",
    )(h, gain.reshape(1, d), w_hi, w_lo)


def _in_proj_kernel(u_ref, w_ref, o_ref):
    acc = jnp.dot(u_ref[...], w_ref[...], preferred_element_type=F32)
    for s in range(o_ref.shape[0]):
        o_ref[s] = acc[:, s * HEAD_DIM:(s + 1) * HEAD_DIM].astype(BF16)


def _in_proj(u, w, layer):
    t, d = u.shape
    n = w.shape[2]
    tm, tn = _tile(t, PROJ_TM), _tile(n, PROJ_TN)
    return pl.pallas_call(
        _in_proj_kernel,
        grid=(n // tn, t // tm),
        in_specs=[pl.BlockSpec((tm, d), lambda j, i: (i, 0)),
                  pl.BlockSpec((None, d, tn), lambda j, i: (layer, 0, j))],
        out_specs=pl.BlockSpec((tn // HEAD_DIM, tm, HEAD_DIM), lambda j, i: (j, i, 0)),
        out_shape=jax.ShapeDtypeStruct((n // HEAD_DIM, t, HEAD_DIM), BF16),
        compiler_params=_params(("arbitrary", "arbitrary"), 48),
        name="in_proj",
    )(u, w)


def _gate_cumsum_kernel(fc_ref, b_ref, tri_ref, o_ref):
    x = fc_ref[...] + b_ref[...]
    log_f = jnp.minimum(x, 0.0) - jnp.log(1.0 + jnp.exp(-jnp.abs(x)))
    tri = tri_ref[...]
    carry = jnp.zeros((x.shape[0], 1), F32)
    for c in range(x.shape[1] // LANES):
        seg = log_f[:, c * LANES:(c + 1) * LANES]
        a = seg.astype(BF16)
        rest = seg - a.astype(F32)
        b, c3 = _split_bf16(rest)
        cs = (jnp.dot(a, tri, preferred_element_type=F32) + jnp.dot(b, tri, preferred_element_type=F32)
              + jnp.dot(c3, tri, preferred_element_type=F32)) + carry
        o_ref[:, c * LANES:(c + 1) * LANES] = cs
        carry = cs[:, LANES - 1:LANES]


def _gate_cumsum(fc_t, b_f, batch):
    r, t = fc_t.shape
    s = t // batch
    idx = jnp.arange(LANES)
    tri = (idx[:, None] <= idx[None, :]).astype(BF16)
    return pl.pallas_call(
        _gate_cumsum_kernel,
        grid=(batch,),
        in_specs=[pl.BlockSpec((r, s), lambda b: (0, b)),
                  pl.BlockSpec((r, 1), lambda b: (0, 0)),
                  pl.BlockSpec((LANES, LANES), lambda b: (0, 0))],
        out_specs=pl.BlockSpec((r, s), lambda b: (0, b)),
        out_shape=jax.ShapeDtypeStruct((r, t), F32),
        compiler_params=_params(("arbitrary",), 32),
        name="gate_cumsum",
    )(fc_t, b_f, tri)


def _kv_rows(j, tk):
    return pl.ds(pl.multiple_of(j * tk, tk), tk)


def _with_ones(v):
    return jnp.concatenate([v, jnp.ones_like(v)], axis=1)


def _row_max(e):
    part = e[:, 0:LANES]
    for c in range(1, e.shape[1] // LANES):
        part = jnp.maximum(part, e[:, c * LANES:(c + 1) * LANES])
    return jnp.max(part, axis=-1, keepdims=True)


def _softmax_first(e, vx, m_ref, acc_ref):
    m = _row_max(e)
    m_ref[...] = jnp.broadcast_to(m, m_ref.shape)
    acc_ref[...] = jnp.dot(jnp.exp2(e - m).astype(BF16), vx, preferred_element_type=F32)


def _softmax_fixed(e, vx, m_ref, acc_ref):
    m = m_ref[...]
    p = jnp.concatenate([jnp.exp2(e[:, c * LANES:(c + 1) * LANES] - m)
                         for c in range(e.shape[1] // LANES)], axis=1)
    acc_ref[...] += jnp.dot(p.astype(BF16), vx, preferred_element_type=F32)


def _softmax_online(e, vx, m_ref, acc_ref):
    m_old = m_ref[:, 0:1]
    m_new = jnp.maximum(m_old, _row_max(e))
    alpha = jnp.exp2(m_old - m_new)
    p = jnp.exp2(e - m_new).astype(BF16)
    acc_ref[...] = alpha * acc_ref[...] + jnp.dot(p, vx, preferred_element_type=F32)
    m_ref[...] = jnp.broadcast_to(m_new, m_ref.shape)


def _sweep_left(i, first_step, safe, step_fn, fixed_count=None):
    count = jnp.maximum(i - first_step + 1, 0)
    near = count if fixed_count is None else jnp.minimum(count, fixed_count)

    @pl.when(safe)
    def _fixed():
        def pair(step, carry):
            step_fn(i - first_step - 1 - 2 * step, 2, _softmax_fixed)
            return carry
        lax.fori_loop(0, lax.shift_right_logical(near, 1), pair, 0)

        @pl.when((near & 1) == 1)
        def _last():
            step_fn(i - first_step + 1 - near, 1, _softmax_fixed)

    @pl.when(jnp.logical_not(safe))
    def _online():
        def single(step, carry):
            step_fn(i - first_step - step, 1, _softmax_online)
            return carry
        lax.fori_loop(0, count, single, 0)


def _softmax_result(acc_ref):
    acc = acc_ref[...]
    return acc[:, :HEAD_DIM] / acc[:, HEAD_DIM:]


def _head_specs(t_blocks_per_batch, tq, seq, q_slot, k_slot, v_slot):
    nq = t_blocks_per_batch
    return [pl.BlockSpec((1, tq, HEAD_DIM), lambda b, h, i: (q_slot + h, b * nq + i, 0)),
            pl.BlockSpec((1, seq, HEAD_DIM), lambda b, h, i: (k_slot + h, b, 0)),
            pl.BlockSpec((1, seq, HEAD_DIM), lambda b, h, i: (v_slot + h, b, 0))]


def _sb_kernel(q_ref, k_ref, v_ref, tri_ref, g_ref, o_ref, acc_ref, run_ref):
    i = pl.program_id(2)
    tq = q_ref.shape[1]
    q = (q_ref[0].astype(F32) * (1.0 / math.sqrt(HEAD_DIM))).astype(BF16)
    row = lax.broadcasted_iota(jnp.int32, (tq, tq), 0)
    col = lax.broadcasted_iota(jnp.int32, (tq, tq), 1)
    strict = col < row
    acc_ref[...] = jnp.zeros_like(acc_ref)
    run_ref[...] = jnp.zeros_like(run_ref)

    def block(j, masked):
        k = k_ref[0, _kv_rows(j, tq), :]
        v = v_ref[0, _kv_rows(j, tq), :]
        z = lax.dot_general(q, k, _NT, preferred_element_type=F32)
        sp = jnp.maximum(z, 0.0) + jnp.log(1.0 + jnp.exp(-jnp.abs(z)))
        if masked:
            sp = jnp.where(strict, sp, 0.0)
        suffix = jnp.dot(sp.astype(BF16), tri_ref[...], preferred_element_type=F32)
        w = jnp.exp(z - suffix - run_ref[...])
        if masked:
            w = jnp.where(strict, w, 0.0)
        acc_ref[...] += jnp.dot(w.astype(BF16), v, preferred_element_type=F32)
        run_ref[...] += suffix[:, 0:1]

    block(i, True)

    def more(carry):
        step, smallest_run = carry
        return jnp.logical_and(step < i, smallest_run < SB_UNDERFLOW)

    def body(carry):
        step, _ = carry
        block(i - 1 - step, False)
        return step + 1, jnp.min(run_ref[...])

    lax.while_loop(more, body, (0, jnp.min(run_ref[...])))
    o_ref[...] = _rms(acc_ref[...], g_ref[...]).astype(BF16)


def _sb_attention(proj, gain, batch, seq):
    t = proj.shape[1]
    tq = _tile(seq, ATT_T)
    nq = seq // tq
    idx = jnp.arange(tq)
    tri = (idx[:, None] >= idx[None, :]).astype(BF16)
    return pl.pallas_call(
        _sb_kernel,
        grid=(batch, H_SB, nq),
        in_specs=_head_specs(nq, tq, seq, 0, H_SB, 2 * H_SB) + [
            pl.BlockSpec((tq, tq), lambda b, h, i: (0, 0)),
            pl.BlockSpec((1, HEAD_DIM), lambda b, h, i: (0, 0))],
        out_specs=pl.BlockSpec((tq, HEAD_DIM), lambda b, h, i: (b * nq + i, h)),
        out_shape=jax.ShapeDtypeStruct((t, H_SB * HEAD_DIM), BF16),
        scratch_shapes=[pltpu.VMEM((tq, HEAD_DIM), F32), pltpu.VMEM((tq, 1), F32)],
        compiler_params=_params(("arbitrary", "arbitrary", "arbitrary"), 40),
        name="sb_attention",
    )(proj, proj, proj, tri, gain.reshape(1, HEAD_DIM))


def _sub_rms(x, gain):
    lo = lax.broadcasted_iota(jnp.int32, x.shape, 1) < HALF_DIM
    sq = x * x
    ms_lo = jnp.sum(jnp.where(lo, sq, 0.0), axis=-1, keepdims=True) * (1.0 / HALF_DIM)
    ms_hi = jnp.sum(jnp.where(lo, 0.0, sq), axis=-1, keepdims=True) * (1.0 / HALF_DIM)
    inv = jnp.where(lo, lax.rsqrt(ms_lo + EPS), lax.rsqrt(ms_hi + EPS))
    return x * inv * gain


def _diff_kernel(lam_init, lam_ref, safe_ref, q_ref, k_ref, v_ref, bias_ref, gq_ref, gk_ref, go_ref, o_ref,
                 kp_ref, acc_ref, m_ref):
    i = pl.program_id(2)
    tq = q_ref.shape[1]
    seq = k_ref.shape[1]

    @pl.when(i == 0)
    def _prep_keys():
        def prep(c, carry):
            rows = _kv_rows(c, KPREP_ROWS)
            kn = _sub_rms(k_ref[0, rows, :].astype(F32), gk_ref[...])
            lo = lax.broadcasted_iota(jnp.int32, kn.shape, 1) < HALF_DIM
            kp_ref[0, rows, :] = jnp.where(lo, kn, 0.0).astype(BF16)
            kp_ref[1, rows, :] = jnp.where(lo, 0.0, kn).astype(BF16)
            return carry
        lax.fori_loop(0, seq // KPREP_ROWS, prep, 0)

    q = (_sub_rms(q_ref[0].astype(F32), gq_ref[...]) * (LOG2E / math.sqrt(HALF_DIM))).astype(BF16)
    safe = safe_ref[pl.program_id(1)] != 0

    def block(j, n, table, update):
        rows = pl.ds(pl.multiple_of(j * tq, tq), n * tq)
        vx = _with_ones(v_ref[0, rows, :])
        for sub in range(2):
            e = lax.dot_general(q, kp_ref[sub, rows, :], _NT, preferred_element_type=F32)
            if table is not None:
                e = e + bias_ref[0, table]
            update(e, vx, m_ref.at[sub], acc_ref.at[sub])

    block(i, 1, 0, _softmax_first)
    for flag, update in ((safe, _softmax_fixed), (jnp.logical_not(safe), _softmax_online)):
        @pl.when(jnp.logical_and(i >= 1, flag))
        def _near(update=update):
            block(i - 1, 1, 1, update)
    _sweep_left(i, 2, safe, lambda j, n, update: block(j, n, None, update))
    o = _softmax_result(acc_ref.at[0]) - lam_ref[0] * _softmax_result(acc_ref.at[1])
    o_ref[...] = (_rms(o, go_ref[...]) * (1.0 - lam_init)).astype(BF16)


def _relative_bucket(rel):
    half = N_BUCKETS // 2
    max_exact = half // 2
    n = jnp.abs(rel)
    large = max_exact + (jnp.log(jnp.maximum(n, 1).astype(F32) / max_exact)
                         / math.log(MAX_DISTANCE / max_exact) * (half - max_exact)).astype(jnp.int32)
    large = jnp.minimum(large, half - 1)
    return jnp.where(rel > 0, half, 0) + jnp.where(n < max_exact, n, large)


def _diff_bias_tables(rel_bias, tq):
    assert tq >= MAX_DISTANCE and tq % CHUNK == 0
    t = jnp.arange(tq)[:, None]
    s = jnp.arange(tq)[None, :]
    rel = jnp.stack([s - t - delta * tq for delta in range(2)])
    bucket = _relative_bucket(rel)[None]
    shifted = ((rel_bias - rel_bias[N_BUCKETS // 2 - 1]) * LOG2E).astype(F32)
    tab = jnp.zeros((rel_bias.shape[1], 2, tq, tq), F32)
    for b in range(N_BUCKETS):
        tab = jnp.where(bucket == b, shifted[b][:, None, None, None], tab)
    visible = jnp.stack([(s // CHUNK) <= (t // CHUNK), jnp.ones((tq, tq), bool)])
    return jnp.where(visible[None], tab, NEG)


def _diff_attention(proj, lam_val, lam_init, bias_tabs, rel_bias, gq, gk, go, batch, seq):
    t = proj.shape[1]
    tq = bias_tabs.shape[-1]
    nq = seq // tq
    base = 3 * H_SB
    vec = pl.BlockSpec((1, HEAD_DIM), lambda b, h, i: (0, 0))
    bound = 2.0 * math.sqrt(HALF_DIM) * jnp.max(jnp.abs(gq)) * jnp.max(jnp.abs(gk))
    spread = jnp.max(rel_bias, axis=0) - jnp.min(rel_bias, axis=0)
    safe = ((bound + spread) * LOG2E < SAFE_EXP2_RANGE).astype(jnp.int32)
    return pl.pallas_call(
        functools.partial(_diff_kernel, lam_init),
        grid=(batch, H_DIFF, nq),
        in_specs=[pl.BlockSpec(memory_space=pltpu.SMEM), pl.BlockSpec(memory_space=pltpu.SMEM)]
        + _head_specs(nq, tq, seq, base, base + H_DIFF, base + 2 * H_DIFF)
        + [pl.BlockSpec((1, 2, tq, tq), lambda b, h, i: (h, 0, 0, 0)), vec, vec, vec],
        out_specs=pl.BlockSpec((tq, HEAD_DIM), lambda b, h, i: (b * nq + i, h)),
        out_shape=jax.ShapeDtypeStruct((t, H_DIFF * HEAD_DIM), BF16),
        scratch_shapes=[pltpu.VMEM((2, seq, HEAD_DIM), BF16), pltpu.VMEM((2, tq, 2 * HEAD_DIM), F32),
                        pltpu.VMEM((2, tq, LANES), F32)],
        compiler_params=_params(("arbitrary", "arbitrary", "arbitrary"), 48),
        name="diff_attention",
    )(lam_val.reshape(1), safe, proj, proj, proj, bias_tabs, gq.reshape(1, HEAD_DIM), gk.reshape(1, HEAD_DIM),
      go.reshape(1, HEAD_DIM))


def _fox_kernel(safe_ref, margin_ref, edge_ref, q_ref, k_ref, v_ref, c_ref, gq_ref, gk_ref, go_ref, o_ref,
                kn_ref, acc_ref, m_ref):
    i = pl.program_id(2)
    tq = q_ref.shape[1]
    seq = k_ref.shape[1]
    nq = seq // tq
    edges = (pl.program_id(0) * GATE_ROWS + pl.program_id(1)) * (2 * nq)
    reach = edge_ref[edges + i] + margin_ref[0]
    live_blocks = jnp.int32(0)
    for j in range(nq):
        live_blocks += jnp.logical_and(j < i, edge_ref[edges + nq + j] <= reach).astype(jnp.int32)

    @pl.when(i == 0)
    def _prep_keys():
        def prep(c, carry):
            rows = _kv_rows(c, KPREP_ROWS)
            kn_ref[rows, :] = _rms(k_ref[0, rows, :].astype(F32), gk_ref[...]).astype(BF16)
            return carry
        lax.fori_loop(0, seq // KPREP_ROWS, prep, 0)

    q = (_rms(q_ref[0].astype(F32), gq_ref[...]) * (LOG2E / math.sqrt(HEAD_DIM))).astype(BF16)
    row = lax.broadcasted_iota(jnp.int32, (tq, tq), 0)
    col = lax.broadcasted_iota(jnp.int32, (tq, tq), 1)
    causal = col <= row
    c_first = c_ref[0, i][:, 0:1]

    def block(j, n, masked, update):
        rows = pl.ds(pl.multiple_of(j * tq, tq), n * tq)
        e = lax.dot_general(q, kn_ref[rows, :], _NT, preferred_element_type=F32)
        c_keys = jnp.concatenate([c_ref[0, j + b] for b in range(n)], axis=1)
        e = e + (c_first - c_keys) * LOG2E
        if masked:
            e = jnp.where(causal, e, NEG)
        update(e, _with_ones(v_ref[0, rows, :]), m_ref, acc_ref)

    block(i, 1, True, _softmax_first)
    _sweep_left(i, 1, safe_ref[0] != 0, lambda j, n, update: block(j, n, False, update), live_blocks)
    o_ref[...] = _rms(_softmax_result(acc_ref), go_ref[...]).astype(BF16)


def _fox_attention(proj, cum, gq, gk, go, batch, seq):
    t = proj.shape[1]
    nq, tq = cum.shape[1], cum.shape[3]
    base = 3 * (H_SB + H_DIFF)
    vec = pl.BlockSpec((1, HEAD_DIM), lambda b, h, i: (0, 0))
    bound = 2.0 * math.sqrt(HEAD_DIM) * jnp.max(jnp.abs(gq)) * jnp.max(jnp.abs(gk))
    safe = (bound * LOG2E < SAFE_EXP2_RANGE).astype(jnp.int32).reshape(1)
    margin = (bound + EXP2_UNDERFLOW / LOG2E).astype(F32).reshape(1)
    edges = jnp.concatenate([cum[:, :, 0, 0], cum[:, :, 0, tq - 1]], axis=1).reshape(-1)
    smem = pl.BlockSpec(memory_space=pltpu.SMEM)
    return pl.pallas_call(
        _fox_kernel,
        grid=(batch, H_FOX, nq),
        in_specs=[smem, smem, smem]
        + _head_specs(nq, tq, seq, base, base + H_FOX, base + 2 * H_FOX)
        + [pl.BlockSpec((1, nq, 1, tq), lambda b, h, i: (b * GATE_ROWS + h, 0, 0, 0)), vec, vec, vec],
        out_specs=pl.BlockSpec((tq, HEAD_DIM), lambda b, h, i: (b * nq + i, h)),
        out_shape=jax.ShapeDtypeStruct((t, H_FOX * HEAD_DIM), BF16),
        scratch_shapes=[pltpu.VMEM((seq, HEAD_DIM), BF16), pltpu.VMEM((tq, 2 * HEAD_DIM), F32),
                        pltpu.VMEM((tq, LANES), F32)],
        compiler_params=_params(("arbitrary", "arbitrary", "arbitrary"), 40),
        name="fox_attention",
    )(safe, margin, edges, proj, proj, proj, cum, gq.reshape(1, HEAD_DIM), gk.reshape(1, HEAD_DIM),
      go.reshape(1, HEAD_DIM))


def _out_proj_kernel(a_sb, a_diff, a_fox, w_ref, h_ref, o_ref, wb_ref):
    @pl.when(pl.program_id(1) == 0)
    def _cast_weights():
        wb_ref[...] = w_ref[...].astype(BF16)

    acc = h_ref[...]
    row = 0
    for a in (a_sb, a_diff, a_fox):
        width = a.shape[1]
        acc += jnp.dot(a[...], wb_ref[row:row + width, :], preferred_element_type=F32)
        row += width
    o_ref[...] = acc


def _out_proj(heads, w, layer, h):
    t, d = h.shape
    tm, tn = _tile(t, OUT_TM), _tile(d, OUT_TN)
    return pl.pallas_call(
        _out_proj_kernel,
        grid=(d // tn, t // tm),
        in_specs=[pl.BlockSpec((tm, a.shape[1]), lambda j, i: (i, 0)) for a in heads]
        + [pl.BlockSpec((None, d, tn), lambda j, i: (layer, 0, j)),
           pl.BlockSpec((tm, tn), lambda j, i: (i, j))],
        out_specs=pl.BlockSpec((tm, tn), lambda j, i: (i, j)),
        out_shape=jax.ShapeDtypeStruct((t, d), F32),
        scratch_shapes=[pltpu.VMEM((d, tn), BF16)],
        compiler_params=_params(("arbitrary", "arbitrary"), 52),
        name="out_proj",
    )(*heads, w, h)


def _route_kernel(x_ref, g_ref, wh_ref, wl_ref, b_ref, eid_ref, gate_ref, u_ref):
    y = _rms(x_ref[...], g_ref[...])
    u_ref[...] = _pack_bf16_pairs(y)
    p, _ = _small_proj_t(wh_ref[...], wl_ref[...], y)
    logits = p + b_ref[...]
    tm = logits.shape[1]

    def softmax0(a):
        ea = jnp.exp(a - jnp.max(a, axis=0, keepdims=True))
        return ea / jnp.sum(ea, axis=0, keepdims=True)

    def first_argmax(a, amax, n):
        rows = lax.broadcasted_iota(jnp.int32, a.shape, 0)
        return jnp.min(jnp.where(a == amax, rows, n), axis=0, keepdims=True)

    p_coarse = softmax0(logits[0:N_GROUPS])
    g_prob = jnp.max(p_coarse, axis=0, keepdims=True)
    g_idx = first_argmax(p_coarse, g_prob, N_GROUPS)
    fine = jnp.zeros((EXPERTS_PER_GROUP, tm), F32)
    for g in range(N_GROUPS):
        r0 = ROUTER_FINE_ROW0 + g * EXPERTS_PER_GROUP
        fine = jnp.where(g_idx == g, logits[r0:r0 + EXPERTS_PER_GROUP], fine)
    p_fine = softmax0(fine)
    p1 = jnp.max(p_fine, axis=0, keepdims=True)
    i1 = first_argmax(p_fine, p1, EXPERTS_PER_GROUP)
    rows = lax.broadcasted_iota(jnp.int32, p_fine.shape, 0)
    rest = jnp.where(rows == i1, -1.0, p_fine)
    p2 = jnp.max(rest, axis=0, keepdims=True)
    i2 = first_argmax(rest, p2, EXPERTS_PER_GROUP)
    denom = p1 + p2
    out_rows = lax.broadcasted_iota(jnp.int32, eid_ref.shape, 0)
    e1 = g_idx * EXPERTS_PER_GROUP + i1
    e2 = g_idx * EXPERTS_PER_GROUP + i2
    eid_ref[...] = jnp.where(out_rows == 0, e1, jnp.where(out_rows == 1, e2, 0))
    gate_ref[...] = jnp.where(out_rows == 0, g_prob * p1 / denom,
                              jnp.where(out_rows == 1, g_prob * p2 / denom, 0.0))


def _route(h, gain, w_hi, w_lo, bias):
    t, d = h.shape
    tm = _tile(t, NORM_TM)
    r = w_hi.shape[0]
    out_rows = 8
    return pl.pallas_call(
        _route_kernel,
        grid=(t // tm,),
        in_specs=[pl.BlockSpec((tm, d), lambda i: (i, 0)),
                  pl.BlockSpec((1, d), lambda i: (0, 0)),
                  pl.BlockSpec((r, d), lambda i: (0, 0)),
                  pl.BlockSpec((r, d), lambda i: (0, 0)),
                  pl.BlockSpec((r, 1), lambda i: (0, 0))],
        out_specs=[pl.BlockSpec((out_rows, tm), lambda i: (0, i)),
                   pl.BlockSpec((out_rows, tm), lambda i: (0, i)),
                   pl.BlockSpec((tm, d // 2), lambda i: (i, 0))],
        out_shape=[jax.ShapeDtypeStruct((out_rows, t), jnp.int32),
                   jax.ShapeDtypeStruct((out_rows, t), F32),
                   jax.ShapeDtypeStruct((t, d // 2), jnp.uint32)],
        compiler_params=_params(("arbitrary",), 40),
        name="route",
    )(h, gain.reshape(1, d), w_hi, w_lo, bias)


def _row_copy(src_hbm, row, dst, slot, sem):
    return pltpu.make_async_copy(src_hbm.at[pl.ds(row, 1)], dst.at[pl.ds(slot, 1)], sem)


def _start_rows(src_hbm, idx_ref, idx_base, dst, n_rows, sem):
    def group(g, carry):
        for lane in range(ROWS_PER_ISSUE):
            r = g * ROWS_PER_ISSUE + lane
            _row_copy(src_hbm, idx_ref[0, 0, idx_base + r], dst, r, sem).start(priority=lane % DMA_QUEUES)
        return carry
    lax.fori_loop(0, n_rows // ROWS_PER_ISSUE, group, 0)


def _wait_rows(src_hbm, dst, n_rows, sem):
    def one(r, carry):
        _row_copy(src_hbm, 0, dst, r, sem).wait()
        return carry
    lax.fori_loop(0, n_rows, one, 0)


def _experts_kernel(be_ref, nu_ref, tok_ref, tok_next_ref, u_hbm, wg_ref, wu_ref, wd_ref, o_ref, x_buf, sems):
    i = pl.program_id(0)
    n_used = nu_ref[0]
    tm = x_buf.shape[1]

    @pl.when(i == 0)
    def _first_block():
        _start_rows(u_hbm, tok_ref, 0, x_buf.at[0], tm, sems.at[0])

    @pl.when(i + 1 < n_used)
    def _next_block():
        nxt = (i + 1) & 1
        _start_rows(u_hbm, tok_next_ref, 0, x_buf.at[nxt], tm, sems.at[nxt])

    @pl.when(i < n_used)
    def _run():
        cur = i & 1
        _wait_rows(u_hbm, x_buf.at[cur], tm, sems.at[cur])
        x = jnp.concatenate(_unpack_bf16_pairs(x_buf[cur]), axis=1).astype(BF16)
        a = jnp.dot(x, wg_ref[...], preferred_element_type=F32)
        b = jnp.dot(x, wu_ref[...], preferred_element_type=F32)
        hid = (a / (1.0 + jnp.exp(-a)) * b).astype(BF16)
        o_ref[...] = _pack_bf16_pairs(jnp.dot(hid, wd_ref[...], preferred_element_type=F32))

    @pl.when(i >= n_used)
    def _unused_tail():
        o_ref[...] = jnp.zeros_like(o_ref)


def _experts(u, row_tok, block_e, n_used, wg, wu, wd, layer):
    n_blocks, _, tm = row_tok.shape
    half = u.shape[1]
    d, f = wg.shape[1], wg.shape[2]

    def w_map(i, be, nu):
        return (layer * N_EXPERTS + be[jnp.minimum(i, nu[0] - 1)], 0, 0)

    return pl.pallas_call(
        _experts_kernel,
        grid_spec=pltpu.PrefetchScalarGridSpec(
            num_scalar_prefetch=2,
            grid=(n_blocks,),
            in_specs=[pl.BlockSpec((1, 1, tm), lambda i, be, nu: (i, 0, 0), memory_space=pltpu.SMEM),
                      pl.BlockSpec((1, 1, tm), lambda i, be, nu: (jnp.minimum(i + 1, n_blocks - 1), 0, 0),
                                   memory_space=pltpu.SMEM),
                      pl.BlockSpec(memory_space=pl.ANY),
                      pl.BlockSpec((None, d, f), w_map),
                      pl.BlockSpec((None, d, f), w_map),
                      pl.BlockSpec((None, f, d), w_map)],
            out_specs=pl.BlockSpec((tm, half), lambda i, be, nu: (i, 0)),
            scratch_shapes=[pltpu.VMEM((2, tm, half), jnp.uint32), pltpu.SemaphoreType.DMA((2,))]),
        out_shape=jax.ShapeDtypeStruct((n_blocks * tm, half), jnp.uint32),
        compiler_params=_params(("arbitrary",), 48),
        name="moe_experts",
    )(block_e, n_used, row_tok, row_tok, u, wg, wu, wd)


def _combine_kernel(d_ref, d_next_ref, g_ref, h_ref, ys_hbm, o_ref, buf, sems):
    i = pl.program_id(0)
    tc, half = buf.shape[2], buf.shape[3]

    def fetch(idx_ref, slot):
        for k in range(TOP_K):
            _start_rows(ys_hbm, idx_ref, k * tc, buf.at[slot, k], tc, sems.at[slot])

    @pl.when(i == 0)
    def _first_block():
        fetch(d_ref, 0)

    @pl.when(i + 1 < pl.num_programs(0))
    def _next_block():
        fetch(d_next_ref, (i + 1) & 1)

    cur = i & 1
    for k in range(TOP_K):
        _wait_rows(ys_hbm, buf.at[cur, k], tc, sems.at[cur])
    g = g_ref[...]
    y_hi = h_ref[:, :half]
    y_lo = h_ref[:, half:]
    for k in range(TOP_K):
        hi, lo = _unpack_bf16_pairs(buf[cur, k])
        y_hi += g[:, k:k + 1] * hi
        y_lo += g[:, k:k + 1] * lo
    o_ref[:, :half] = y_hi
    o_ref[:, half:] = y_lo


def _combine(h, ys, dest, gates):
    t, d = h.shape
    n_tb, _, two_tc = dest.shape
    tc = two_tc // TOP_K
    return pl.pallas_call(
        _combine_kernel,
        grid=(n_tb,),
        in_specs=[pl.BlockSpec((1, 1, two_tc), lambda i: (i, 0, 0), memory_space=pltpu.SMEM),
                  pl.BlockSpec((1, 1, two_tc), lambda i: (jnp.minimum(i + 1, n_tb - 1), 0, 0),
                               memory_space=pltpu.SMEM),
                  pl.BlockSpec((tc, TOP_K), lambda i: (i, 0)),
                  pl.BlockSpec((tc, d), lambda i: (i, 0)),
                  pl.BlockSpec(memory_space=pl.ANY)],
        out_specs=pl.BlockSpec((tc, d), lambda i: (i, 0)),
        out_shape=jax.ShapeDtypeStruct((t, d), F32),
        scratch_shapes=[pltpu.VMEM((2, TOP_K, tc, d // 2), jnp.uint32), pltpu.SemaphoreType.DMA((2,))],
        compiler_params=_params(("arbitrary",), 40),
        name="moe_combine",
    )(dest, dest, gates, h, ys)


def _dispatch_plan(eid, t, tm, tc):
    flat_e = eid[:TOP_K].reshape(-1)
    onehot = (flat_e[:, None] == jnp.arange(N_EXPERTS, dtype=jnp.int32)[None, :]).astype(jnp.int32)
    csum = jnp.cumsum(onehot, axis=0)
    counts = csum[-1]
    rank = jnp.sum(onehot * csum, axis=1) - 1
    padded = (counts + tm - 1) // tm * tm
    pends = jnp.cumsum(padded)
    pstarts = pends - padded
    dest = jnp.sum(onehot * pstarts[None, :], axis=1) + rank
    n_blocks = -(-(TOP_K * t + N_EXPERTS * (tm - 1)) // tm)
    flat_tok = jnp.tile(jnp.arange(t, dtype=jnp.int32), TOP_K)
    row_tok = jnp.zeros((n_blocks * tm,), jnp.int32).at[dest].set(flat_tok, unique_indices=True)
    starts = jnp.arange(n_blocks, dtype=jnp.int32) * tm
    block_e = jnp.minimum(jnp.sum((starts[:, None] >= pends[None, :]).astype(jnp.int32), axis=1),
                          N_EXPERTS - 1).astype(jnp.int32)
    n_used = (pends[-1] // tm).astype(jnp.int32).reshape(1)
    dest_blocks = dest.reshape(TOP_K, t // tc, tc).transpose(1, 0, 2).reshape(t // tc, 1, TOP_K * tc)
    return row_tok.reshape(n_blocks, 1, tm), block_e, n_used, dest_blocks.astype(jnp.int32)


def _hi_lo_rows(w_t, rows):
    padded = jnp.zeros((rows, w_t.shape[1]), F32).at[:w_t.shape[0]].set(w_t)
    return _split_bf16(padded)


def _token_mixers(h, layer, bias_tabs, rel_bias, norm_attn, w_qkv, w_gates, b_f, q_gain_diff, k_gain_diff, lam,
                  subln_gain, q_gain_fox, k_gain_fox, out_gain_sb, out_gain_fox, w_out, batch, seq):
    wf_hi, wf_lo = _hi_lo_rows(w_gates.T, GATE_ROWS)
    u, fc_t = _attn_norm(h, norm_attn, wf_hi, wf_lo)
    proj = _in_proj(u, w_qkv, layer)

    b_pad = jnp.zeros((GATE_ROWS, 1), F32).at[:H_FOX, 0].set(b_f)
    cum_t = _gate_cumsum(fc_t, b_pad, batch)
    tq = bias_tabs.shape[-1]
    nq = seq // tq
    cum = cum_t.reshape(GATE_ROWS, batch, nq, 1, tq).transpose(1, 0, 2, 3, 4).reshape(batch * GATE_ROWS, nq, 1, tq)

    lam_init = 0.8 - 0.6 * math.exp(-0.3 * layer)
    lamf = lam.astype(F32)
    lam_val = jnp.exp(jnp.sum(lamf[0] * lamf[1])) - jnp.exp(jnp.sum(lamf[2] * lamf[3])) + lam_init

    o_sb = _sb_attention(proj, out_gain_sb, batch, seq)
    o_diff = _diff_attention(proj, lam_val, lam_init, bias_tabs, rel_bias, q_gain_diff.reshape(-1),
                             k_gain_diff.reshape(-1), subln_gain, batch, seq)
    o_fox = _fox_attention(proj, cum, q_gain_fox, k_gain_fox, out_gain_fox, batch, seq)

    return _out_proj([o_sb, o_diff, o_fox], w_out, layer, h)


def _moe(h, layer, norm_ffn, w_coarse, b_coarse, w_fine, b_fine, w_gate, w_up, w_down):
    t, d = h.shape
    assert EXPERTS_PER_GROUP == 8 and N_GROUPS <= ROUTER_FINE_ROW0 and TOP_K == 2
    w_router = jnp.zeros((ROUTER_ROWS, d), F32)
    w_router = w_router.at[:N_GROUPS].set(w_coarse.T)
    w_router = w_router.at[ROUTER_FINE_ROW0:ROUTER_FINE_ROW0 + N_EXPERTS].set(w_fine.T)
    b_router = jnp.zeros((ROUTER_ROWS, 1), F32)
    b_router = b_router.at[:N_GROUPS, 0].set(b_coarse)
    b_router = b_router.at[ROUTER_FINE_ROW0:ROUTER_FINE_ROW0 + N_EXPERTS, 0].set(b_fine)
    wr_hi, wr_lo = _split_bf16(w_router)
    eid, gate, u = _route(h, norm_ffn, wr_hi, wr_lo, b_router)

    tm = MOE_TM
    tc = _tile(t, COMBINE_TC, 8)
    row_tok, block_e, n_used, dest = _dispatch_plan(eid, t, tm, tc)
    ys = _experts(u, row_tok, block_e, n_used, w_gate, w_up, w_down, layer)
    return _combine(h, ys, dest, gate[:TOP_K].T)


def kernel(x, norm_attn, w_in, b_f, q_gain_diff, k_gain_diff, lam, subln_gain, q_gain_fox, k_gain_fox,
           out_gain_sb, out_gain_fox, rel_bias, w_out, norm_ffn, w_coarse, b_coarse, w_fine, b_fine,
           w_gate, w_up, w_down):
    batch, seq, d = x.shape
    h = x.reshape(batch * seq, d)
    bias_tabs = _diff_bias_tables(rel_bias, _tile(seq, ATT_T))
    experts = [w.astype(BF16).reshape((-1,) + w.shape[2:]) for w in (w_gate, w_up, w_down)]
    n_qkv = 3 * D_MODEL
    w_qkv = w_in[:, :, :n_qkv].astype(BF16)
    for layer in range(DEPTH):
        h = _token_mixers(h, layer, bias_tabs, rel_bias, norm_attn[layer], w_qkv, w_in[layer, :, n_qkv:], b_f[layer],
                          q_gain_diff[layer], k_gain_diff[layer], lam[layer], subln_gain[layer],
                          q_gain_fox[layer], k_gain_fox[layer], out_gain_sb[layer], out_gain_fox[layer],
                          w_out, batch, seq)
        h = _moe(h, layer, norm_ffn[layer], w_coarse[layer], b_coarse[layer], w_fine[layer], b_fine[layer],
                 *experts)
    return h.reshape(batch, seq, d)
```

```python
import functools
import math

import jax
import jax.numpy as jnp
from jax import lax
from jax.experimental import pallas as pl
from jax.experimental.pallas import tpu as pltpu

D_MODEL = 4096
BATCH = 2
SEQ = 8192
DEPTH = 2
CHUNK = 64
HEAD_DIM = 128
HALF_DIM = HEAD_DIM // 2
N_HEADS = D_MODEL // HEAD_DIM
H_DIFF = (N_HEADS * 5) // 16
H_FOX = (N_HEADS * 5) // 16
H_SB = N_HEADS - H_DIFF - H_FOX
N_BUCKETS = 32
MAX_DISTANCE = 128
N_GROUPS = 4
EXPERTS_PER_GROUP = 8
N_EXPERTS = N_GROUPS * EXPERTS_PER_GROUP
TOP_K = 2
D_EXPERT = 512
EPS = 1e-6

F32 = jnp.float32
BF16 = jnp.bfloat16
NEG = -1e30
LOG2E = math.log2(math.e)
SB_UNDERFLOW = 104.0
SAFE_EXP2_RANGE = 64.0
EXP2_UNDERFLOW = 150.0
SUBLANES_BF16 = 16
LANES = 128
GATE_ROWS = 16
ROUTER_FINE_ROW0 = 8
ROUTER_ROWS = 48

NORM_TM = 512
PROJ_TM = 1024
PROJ_TN = 1024
ATT_T = 512
KPREP_ROWS = 512
OUT_TM = 1024
OUT_TN = 512
MOE_TM = 256
ROWS_PER_ISSUE = 8
DMA_QUEUES = 2
COMBINE_TC = 256

_NT = (((1,), (1,)), ((), ()))


def _params(semantics, vmem_mib):
    return pltpu.CompilerParams(dimension_semantics=semantics, vmem_limit_bytes=vmem_mib << 20)


def _tile(n, pref, unit=LANES):
    t = min(pref, n) // unit * unit
    while n % t:
        t -= unit
    return t


def _rms(x, gain):
    return x * lax.rsqrt(jnp.mean(x * x, axis=-1, keepdims=True) + EPS) * gain


def _split_bf16(a):
    hi = a.astype(BF16)
    lo = (a - hi.astype(F32)).astype(BF16)
    return hi, lo


_HIGH_HALF = 0xFFFF0000


def _pack_bf16_pairs(y):
    half = y.shape[1] // 2
    hi = lax.bitcast_convert_type(y[:, :half].astype(BF16).astype(F32), jnp.uint32)
    lo = lax.bitcast_convert_type(y[:, half:].astype(BF16).astype(F32), jnp.uint32)
    return (hi & jnp.uint32(_HIGH_HALF)) | (lo >> 16)


def _unpack_bf16_pairs(w):
    return (lax.bitcast_convert_type(w & jnp.uint32(_HIGH_HALF), F32),
            lax.bitcast_convert_type(w << 16, F32))


def _small_proj_t(w_hi, w_lo, y):
    y_hi, y_lo = _split_bf16(y)
    p = lax.dot_general(w_hi, y_hi, _NT, preferred_element_type=F32)
    p += lax.dot_general(w_lo, y_hi, _NT, preferred_element_type=F32)
    p += lax.dot_general(w_hi, y_lo, _NT, preferred_element_type=F32)
    return p, y_hi


def _attn_norm_kernel(x_ref, g_ref, wh_ref, wl_ref, u_ref, p_ref):
    y = _rms(x_ref[...], g_ref[...])
    p, y_hi = _small_proj_t(wh_ref[...], wl_ref[...], y)
    u_ref[...] = y_hi
    p_ref[...] = p


def _attn_norm(h, gain, w_hi, w_lo):
    t, d = h.shape
    tm = _tile(t, NORM_TM)
    r = w_hi.shape[0]
    return pl.pallas_call(
        _attn_norm_kernel,
        grid=(t // tm,),
        in_specs=[pl.BlockSpec((tm, d), lambda i: (i, 0)),
                  pl.BlockSpec((1, d), lambda i: (0, 0)),
                  pl.BlockSpec((r, d), lambda i: (0, 0)),
                  pl.BlockSpec((r, d), lambda i: (0, 0))],
        out_specs=[pl.BlockSpec((tm, d), lambda i: (i, 0)),
                   pl.BlockSpec((r, tm), lambda i: (0, i))],
        out_shape=[jax.ShapeDtypeStruct((t, d), BF16), jax.ShapeDtypeStruct((r, t), F32)],
        compiler_params=_params(("arbitrary",), 40),
        name="attn_norm",
    )(h, gain.reshape(1, d), w_hi, w_lo)


def _in_proj_kernel(u_ref, w_ref, o_ref):
    acc = jnp.dot(u_ref[...], w_ref[...], preferred_element_type=F32)
    for s in range(o_ref.shape[0]):
        o_ref[s] = acc[:, s * HEAD_DIM:(s + 1) * HEAD_DIM].astype(BF16)


def _in_proj(u, w, layer):
    t, d = u.shape
    n = w.shape[2]
    tm, tn = _tile(t, PROJ_TM), _tile(n, PROJ_TN)
    return pl.pallas_call(
        _in_proj_kernel,
        grid=(n // tn, t // tm),
        in_specs=[pl.BlockSpec((tm, d), lambda j, i: (i, 0)),
                  pl.BlockSpec((None, d, tn), lambda j, i: (layer, 0, j))],
        out_specs=pl.BlockSpec((tn // HEAD_DIM, tm, HEAD_DIM), lambda j, i: (j, i, 0)),
        out_shape=jax.ShapeDtypeStruct((n // HEAD_DIM, t, HEAD_DIM), BF16),
        compiler_params=_params(("arbitrary", "arbitrary"), 48),
        name="in_proj",
    )(u, w)


def _gate_cumsum_kernel(fc_ref, b_ref, tri_ref, o_ref):
    x = fc_ref[...] + b_ref[...]
    log_f = jnp.minimum(x, 0.0) - jnp.log(1.0 + jnp.exp(-jnp.abs(x)))
    tri = tri_ref[...]
    carry = jnp.zeros((x.shape[0], 1), F32)
    for c in range(x.shape[1] // LANES):
        seg = log_f[:, c * LANES:(c + 1) * LANES]
        a = seg.astype(BF16)
        rest = seg - a.astype(F32)
        b, c3 = _split_bf16(rest)
        cs = (jnp.dot(a, tri, preferred_element_type=F32) + jnp.dot(b, tri, preferred_element_type=F32)
              + jnp.dot(c3, tri, preferred_element_type=F32)) + carry
        o_ref[:, c * LANES:(c + 1) * LANES] = cs
        carry = cs[:, LANES - 1:LANES]


def _gate_cumsum(fc_t, b_f, batch):
    r, t = fc_t.shape
    s = t // batch
    idx = jnp.arange(LANES)
    tri = (idx[:, None] <= idx[None, :]).astype(BF16)
    return pl.pallas_call(
        _gate_cumsum_kernel,
        grid=(batch,),
        in_specs=[pl.BlockSpec((r, s), lambda b: (0, b)),
                  pl.BlockSpec((r, 1), lambda b: (0, 0)),
                  pl.BlockSpec((LANES, LANES), lambda b: (0, 0))],
        out_specs=pl.BlockSpec((r, s), lambda b: (0, b)),
        out_shape=jax.ShapeDtypeStruct((r, t), F32),
        compiler_params=_params(("arbitrary",), 32),
        name="gate_cumsum",
    )(fc_t, b_f, tri)


def _kv_rows(j, tk):
    return pl.ds(pl.multiple_of(j * tk, tk), tk)


def _with_ones(v):
    return jnp.concatenate([v, jnp.ones_like(v)], axis=1)


def _row_max(e):
    part = e[:, 0:LANES]
    for c in range(1, e.shape[1] // LANES):
        part = jnp.maximum(part, e[:, c * LANES:(c + 1) * LANES])
    return jnp.max(part, axis=-1, keepdims=True)


def _softmax_first(e, vx, m_ref, acc_ref):
    m = _row_max(e)
    m_ref[...] = jnp.broadcast_to(m, m_ref.shape)
    acc_ref[...] = jnp.dot(jnp.exp2(e - m).astype(BF16), vx, preferred_element_type=F32)


def _softmax_fixed(e, vx, m_ref, acc_ref):
    m = m_ref[...]
    p = jnp.concatenate([jnp.exp2(e[:, c * LANES:(c + 1) * LANES] - m)
                         for c in range(e.shape[1] // LANES)], axis=1)
    acc_ref[...] += jnp.dot(p.astype(BF16), vx, preferred_element_type=F32)


def _softmax_fixed_assign(e, vx, m_ref, acc_ref):
    m = m_ref[...]
    p = jnp.concatenate([jnp.exp2(e[:, c * LANES:(c + 1) * LANES] - m)
                         for c in range(e.shape[1] // LANES)], axis=1)
    acc_ref[...] = jnp.dot(p.astype(BF16), vx, preferred_element_type=F32)


def _softmax_online(e, vx, m_ref, acc_ref):
    m_old = m_ref[:, 0:1]
    m_new = jnp.maximum(m_old, _row_max(e))
    alpha = jnp.exp2(m_old - m_new)
    p = jnp.exp2(e - m_new).astype(BF16)
    acc_ref[...] = alpha * acc_ref[...] + jnp.dot(p, vx, preferred_element_type=F32)
    m_ref[...] = jnp.broadcast_to(m_new, m_ref.shape)


def _sweep_left(i, first_step, safe, step_fn, fixed_count=None):
    top = i - first_step
    count = jnp.maximum(top + 1, 0)
    near = count if fixed_count is None else jnp.minimum(count, fixed_count)

    @pl.when(safe)
    def _fixed():
        quads = lax.shift_right_logical(near, 2)

        def quad(step, carry):
            step_fn(top - 3 - 4 * step, 4, _softmax_fixed)
            return carry
        lax.fori_loop(0, quads, quad, 0)
        rest_top = top - 4 * quads

        @pl.when((near & 2) != 0)
        def _pair():
            step_fn(rest_top - 1, 2, _softmax_fixed)

        @pl.when((near & 1) != 0)
        def _single():
            step_fn(rest_top - (near & 2), 1, _softmax_fixed)

    @pl.when(jnp.logical_not(safe))
    def _online():
        def single(step, carry):
            step_fn(i - first_step - step, 1, _softmax_online)
            return carry
        lax.fori_loop(0, count, single, 0)


def _softmax_result(acc_ref):
    acc = acc_ref[...]
    return acc[:, :HEAD_DIM] / acc[:, HEAD_DIM:]


def _head_specs(t_blocks_per_batch, tq, seq, q_slot, k_slot, v_slot):
    nq = t_blocks_per_batch
    return [pl.BlockSpec((1, tq, HEAD_DIM), lambda b, h, i: (q_slot + h, b * nq + i, 0)),
            pl.BlockSpec((1, seq, HEAD_DIM), lambda b, h, i: (k_slot + h, b, 0)),
            pl.BlockSpec((1, seq, HEAD_DIM), lambda b, h, i: (v_slot + h, b, 0))]


def _sb_kernel(q_ref, k_ref, v_ref, tri_ref, g_ref, o_ref, acc_ref, run_ref):
    i = pl.program_id(2)
    tq = q_ref.shape[1]
    q = (q_ref[0].astype(F32) * (1.0 / math.sqrt(HEAD_DIM))).astype(BF16)
    row = lax.broadcasted_iota(jnp.int32, (tq, tq), 0)
    col = lax.broadcasted_iota(jnp.int32, (tq, tq), 1)
    strict = col < row
    acc_ref[...] = jnp.zeros_like(acc_ref)
    run_ref[...] = jnp.zeros_like(run_ref)

    def block(j, masked):
        k = k_ref[0, _kv_rows(j, tq), :]
        v = v_ref[0, _kv_rows(j, tq), :]
        z = lax.dot_general(q, k, _NT, preferred_element_type=F32)
        sp = jnp.maximum(z, 0.0) + jnp.log(1.0 + jnp.exp(-jnp.abs(z)))
        if masked:
            sp = jnp.where(strict, sp, 0.0)
        suffix = jnp.dot(sp.astype(BF16), tri_ref[...], preferred_element_type=F32)
        w = jnp.exp(z - suffix - run_ref[...])
        if masked:
            w = jnp.where(strict, w, 0.0)
        acc_ref[...] += jnp.dot(w.astype(BF16), v, preferred_element_type=F32)
        run_ref[...] += suffix[:, 0:1]

    block(i, True)

    def more(carry):
        step, smallest_run = carry
        return jnp.logical_and(step < i, smallest_run < SB_UNDERFLOW)

    def body(carry):
        step, _ = carry
        block(i - 1 - step, False)
        return step + 1, jnp.min(run_ref[...])

    lax.while_loop(more, body, (0, jnp.min(run_ref[...])))
    o_ref[...] = _rms(acc_ref[...], g_ref[...]).astype(BF16)


def _sb_attention(proj, gain, batch, seq):
    t = proj.shape[1]
    tq = _tile(seq, ATT_T)
    nq = seq // tq
    idx = jnp.arange(tq)
    tri = (idx[:, None] >= idx[None, :]).astype(BF16)
    return pl.pallas_call(
        _sb_kernel,
        grid=(batch, H_SB, nq),
        in_specs=_head_specs(nq, tq, seq, 0, H_SB, 2 * H_SB) + [
            pl.BlockSpec((tq, tq), lambda b, h, i: (0, 0)),
            pl.BlockSpec((1, HEAD_DIM), lambda b, h, i: (0, 0))],
        out_specs=pl.BlockSpec((tq, HEAD_DIM), lambda b, h, i: (b * nq + i, h)),
        out_shape=jax.ShapeDtypeStruct((t, H_SB * HEAD_DIM), BF16),
        scratch_shapes=[pltpu.VMEM((tq, HEAD_DIM), F32), pltpu.VMEM((tq, 1), F32)],
        compiler_params=_params(("arbitrary", "arbitrary", "arbitrary"), 40),
        name="sb_attention",
    )(proj, proj, proj, tri, gain.reshape(1, HEAD_DIM))


def _sub_rms(x, gain):
    lo = lax.broadcasted_iota(jnp.int32, x.shape, 1) < HALF_DIM
    sq = x * x
    ms_lo = jnp.sum(jnp.where(lo, sq, 0.0), axis=-1, keepdims=True) * (1.0 / HALF_DIM)
    ms_hi = jnp.sum(jnp.where(lo, 0.0, sq), axis=-1, keepdims=True) * (1.0 / HALF_DIM)
    inv = jnp.where(lo, lax.rsqrt(ms_lo + EPS), lax.rsqrt(ms_hi + EPS))
    return x * inv * gain


def _diff_kernel(lam_init, lam_ref, safe_ref, q_ref, k_ref, v_ref, bias_ref, gq_ref, gk_ref, go_ref, o_ref,
                 kp_ref, acc_ref, m_ref):
    i = pl.program_id(2)
    tq = q_ref.shape[1]
    seq = k_ref.shape[1]

    @pl.when(i == 0)
    def _prep_keys():
        def prep(c, carry):
            rows = _kv_rows(c, KPREP_ROWS)
            kn = _sub_rms(k_ref[0, rows, :].astype(F32), gk_ref[...])
            lo = lax.broadcasted_iota(jnp.int32, kn.shape, 1) < HALF_DIM
            kp_ref[0, rows, :] = jnp.where(lo, kn, 0.0).astype(BF16)
            kp_ref[1, rows, :] = jnp.where(lo, 0.0, kn).astype(BF16)
            return carry
        lax.fori_loop(0, seq // KPREP_ROWS, prep, 0)

    q = (_sub_rms(q_ref[0].astype(F32), gq_ref[...]) * (LOG2E / math.sqrt(HALF_DIM))).astype(BF16)
    safe = safe_ref[pl.program_id(1)] != 0

    def block(j, n, table, update):
        rows = pl.ds(pl.multiple_of(j * tq, tq), n * tq)
        vx = _with_ones(v_ref[0, rows, :])
        for sub in range(2):
            e = lax.dot_general(q, kp_ref[sub, rows, :], _NT, preferred_element_type=F32)
            if table is not None:
                e = e + bias_ref[0, table]
            update(e, vx, m_ref.at[sub], acc_ref.at[sub])

    @pl.when(safe)
    def _diag_fixed():
        ones = jnp.ones((HEAD_DIM, LANES), BF16)
        for sub in range(2):
            qk = q.astype(F32) * kp_ref[sub, _kv_rows(i, tq), :].astype(F32)
            m_ref[sub] = jnp.dot(qk.astype(BF16), ones, preferred_element_type=F32)
        block(i, 1, 0, _softmax_fixed_assign)

    @pl.when(jnp.logical_not(safe))
    def _diag_online():
        block(i, 1, 0, _softmax_first)

    for flag, update in ((safe, _softmax_fixed), (jnp.logical_not(safe), _softmax_online)):
        @pl.when(jnp.logical_and(i >= 1, flag))
        def _near(update=update):
            block(i - 1, 1, 1, update)
    _sweep_left(i, 2, safe, lambda j, n, update: block(j, n, None, update))
    o = _softmax_result(acc_ref.at[0]) - lam_ref[0] * _softmax_result(acc_ref.at[1])
    o_ref[...] = (_rms(o, go_ref[...]) * (1.0 - lam_init)).astype(BF16)


def _relative_bucket(rel):
    half = N_BUCKETS // 2
    max_exact = half // 2
    n = jnp.abs(rel)
    large = max_exact + (jnp.log(jnp.maximum(n, 1).astype(F32) / max_exact)
                         / math.log(MAX_DISTANCE / max_exact) * (half - max_exact)).astype(jnp.int32)
    large = jnp.minimum(large, half - 1)
    return jnp.where(rel > 0, half, 0) + jnp.where(n < max_exact, n, large)


def _diff_bias_tables(rel_bias, tq):
    assert tq >= MAX_DISTANCE and tq % CHUNK == 0
    t = jnp.arange(tq)[:, None]
    s = jnp.arange(tq)[None, :]
    rel = jnp.stack([s - t - delta * tq for delta in range(2)])
    bucket = _relative_bucket(rel)[None]
    shifted = ((rel_bias - rel_bias[N_BUCKETS // 2 - 1]) * LOG2E).astype(F32)
    tab = jnp.zeros((rel_bias.shape[1], 2, tq, tq), F32)
    for b in range(N_BUCKETS):
        tab = jnp.where(bucket == b, shifted[b][:, None, None, None], tab)
    visible = jnp.stack([(s // CHUNK) <= (t // CHUNK), jnp.ones((tq, tq), bool)])
    return jnp.where(visible[None], tab, NEG)


def _diff_attention(proj, lam_val, lam_init, bias_tabs, rel_bias, gq, gk, go, batch, seq):
    t = proj.shape[1]
    tq = bias_tabs.shape[-1]
    nq = seq // tq
    base = 3 * H_SB
    vec = pl.BlockSpec((1, HEAD_DIM), lambda b, h, i: (0, 0))
    bound = 2.0 * math.sqrt(HALF_DIM) * jnp.max(jnp.abs(gq)) * jnp.max(jnp.abs(gk))
    spread = jnp.max(rel_bias, axis=0) - jnp.min(rel_bias, axis=0)
    safe = ((bound + spread) * LOG2E < SAFE_EXP2_RANGE).astype(jnp.int32)
    return pl.pallas_call(
        functools.partial(_diff_kernel, lam_init),
        grid=(batch, H_DIFF, nq),
        in_specs=[pl.BlockSpec(memory_space=pltpu.SMEM), pl.BlockSpec(memory_space=pltpu.SMEM)]
        + _head_specs(nq, tq, seq, base, base + H_DIFF, base + 2 * H_DIFF)
        + [pl.BlockSpec((1, 2, tq, tq), lambda b, h, i: (h, 0, 0, 0)), vec, vec, vec],
        out_specs=pl.BlockSpec((tq, HEAD_DIM), lambda b, h, i: (b * nq + i, h)),
        out_shape=jax.ShapeDtypeStruct((t, H_DIFF * HEAD_DIM), BF16),
        scratch_shapes=[pltpu.VMEM((2, seq, HEAD_DIM), BF16), pltpu.VMEM((2, tq, 2 * HEAD_DIM), F32),
                        pltpu.VMEM((2, tq, LANES), F32)],
        compiler_params=_params(("arbitrary", "arbitrary", "arbitrary"), 48),
        name="diff_attention",
    )(lam_val.reshape(1), safe, proj, proj, proj, bias_tabs, gq.reshape(1, HEAD_DIM), gk.reshape(1, HEAD_DIM),
      go.reshape(1, HEAD_DIM))


def _fox_kernel(safe_ref, margin_ref, edge_ref, q_ref, k_ref, v_ref, c_ref, gq_ref, gk_ref, go_ref, o_ref,
                kn_ref, acc_ref, m_ref):
    i = pl.program_id(2)
    tq = q_ref.shape[1]
    seq = k_ref.shape[1]
    nq = seq // tq
    edges = (pl.program_id(0) * GATE_ROWS + pl.program_id(1)) * (2 * nq)
    reach = edge_ref[edges + i] + margin_ref[0]
    live_blocks = jnp.int32(0)
    for j in range(nq):
        live_blocks += jnp.logical_and(j < i, edge_ref[edges + nq + j] <= reach).astype(jnp.int32)

    @pl.when(i == 0)
    def _prep_keys():
        def prep(c, carry):
            rows = _kv_rows(c, KPREP_ROWS)
            kn_ref[rows, :] = _rms(k_ref[0, rows, :].astype(F32), gk_ref[...]).astype(BF16)
            return carry
        lax.fori_loop(0, seq // KPREP_ROWS, prep, 0)

    q = (_rms(q_ref[0].astype(F32), gq_ref[...]) * (LOG2E / math.sqrt(HEAD_DIM))).astype(BF16)
    row = lax.broadcasted_iota(jnp.int32, (tq, tq), 0)
    col = lax.broadcasted_iota(jnp.int32, (tq, tq), 1)
    causal = col <= row
    c_first = c_ref[0, i][:, 0:1]

    def block(j, n, masked, update):
        rows = pl.ds(pl.multiple_of(j * tq, tq), n * tq)
        e = lax.dot_general(q, kn_ref[rows, :], _NT, preferred_element_type=F32)
        c_keys = jnp.concatenate([c_ref[0, j + b] for b in range(n)], axis=1)
        e = e + (c_first - c_keys) * LOG2E
        if masked:
            e = jnp.where(causal, e, NEG)
        update(e, _with_ones(v_ref[0, rows, :]), m_ref, acc_ref)

    block(i, 1, True, _softmax_first)
    _sweep_left(i, 1, safe_ref[0] != 0, lambda j, n, update: block(j, n, False, update), live_blocks)
    o_ref[...] = _rms(_softmax_result(acc_ref), go_ref[...]).astype(BF16)


def _fox_attention(proj, cum, gq, gk, go, batch, seq):
    t = proj.shape[1]
    nq, tq = cum.shape[1], cum.shape[3]
    base = 3 * (H_SB + H_DIFF)
    vec = pl.BlockSpec((1, HEAD_DIM), lambda b, h, i: (0, 0))
    bound = 2.0 * math.sqrt(HEAD_DIM) * jnp.max(jnp.abs(gq)) * jnp.max(jnp.abs(gk))
    safe = (bound * LOG2E < SAFE_EXP2_RANGE).astype(jnp.int32).reshape(1)
    margin = (bound + EXP2_UNDERFLOW / LOG2E).astype(F32).reshape(1)
    edges = jnp.concatenate([cum[:, :, 0, 0], cum[:, :, 0, tq - 1]], axis=1).reshape(-1)
    smem = pl.BlockSpec(memory_space=pltpu.SMEM)
    return pl.pallas_call(
        _fox_kernel,
        grid=(batch, H_FOX, nq),
        in_specs=[smem, smem, smem]
        + _head_specs(nq, tq, seq, base, base + H_FOX, base + 2 * H_FOX)
        + [pl.BlockSpec((1, nq, 1, tq), lambda b, h, i: (b * GATE_ROWS + h, 0, 0, 0)), vec, vec, vec],
        out_specs=pl.BlockSpec((tq, HEAD_DIM), lambda b, h, i: (b * nq + i, h)),
        out_shape=jax.ShapeDtypeStruct((t, H_FOX * HEAD_DIM), BF16),
        scratch_shapes=[pltpu.VMEM((seq, HEAD_DIM), BF16), pltpu.VMEM((tq, 2 * HEAD_DIM), F32),
                        pltpu.VMEM((tq, LANES), F32)],
        compiler_params=_params(("arbitrary", "arbitrary", "arbitrary"), 40),
        name="fox_attention",
    )(safe, margin, edges, proj, proj, proj, cum, gq.reshape(1, HEAD_DIM), gk.reshape(1, HEAD_DIM),
      go.reshape(1, HEAD_DIM))


def _out_proj_kernel(a_sb, a_diff, a_fox, w_ref, h_ref, o_ref, wb_ref):
    @pl.when(pl.program_id(1) == 0)
    def _cast_weights():
        wb_ref[...] = w_ref[...].astype(BF16)

    acc = h_ref[...]
    row = 0
    for a in (a_sb, a_diff, a_fox):
        width = a.shape[1]
        acc += jnp.dot(a[...], wb_ref[row:row + width, :], preferred_element_type=F32)
        row += width
    o_ref[...] = acc


def _out_proj(heads, w, layer, h):
    t, d = h.shape
    tm, tn = _tile(t, OUT_TM), _tile(d, OUT_TN)
    return pl.pallas_call(
        _out_proj_kernel,
        grid=(d // tn, t // tm),
        in_specs=[pl.BlockSpec((tm, a.shape[1]), lambda j, i: (i, 0)) for a in heads]
        + [pl.BlockSpec((None, d, tn), lambda j, i: (layer, 0, j)),
           pl.BlockSpec((tm, tn), lambda j, i: (i, j))],
        out_specs=pl.BlockSpec((tm, tn), lambda j, i: (i, j)),
        out_shape=jax.ShapeDtypeStruct((t, d), F32),
        scratch_shapes=[pltpu.VMEM((d, tn), BF16)],
        compiler_params=_params(("arbitrary", "arbitrary"), 52),
        name="out_proj",
    )(*heads, w, h)


def _route_kernel(x_ref, g_ref, wh_ref, wl_ref, b_ref, eid_ref, gate_ref, u_ref):
    y = _rms(x_ref[...], g_ref[...])
    u_ref[...] = _pack_bf16_pairs(y)
    p, _ = _small_proj_t(wh_ref[...], wl_ref[...], y)
    logits = p + b_ref[...]
    tm = logits.shape[1]

    def softmax0(a):
        ea = jnp.exp(a - jnp.max(a, axis=0, keepdims=True))
        return ea / jnp.sum(ea, axis=0, keepdims=True)

    def first_argmax(a, amax, n):
        rows = lax.broadcasted_iota(jnp.int32, a.shape, 0)
        return jnp.min(jnp.where(a == amax, rows, n), axis=0, keepdims=True)

    p_coarse = softmax0(logits[0:N_GROUPS])
    g_prob = jnp.max(p_coarse, axis=0, keepdims=True)
    g_idx = first_argmax(p_coarse, g_prob, N_GROUPS)
    fine = jnp.zeros((EXPERTS_PER_GROUP, tm), F32)
    for g in range(N_GROUPS):
        r0 = ROUTER_FINE_ROW0 + g * EXPERTS_PER_GROUP
        fine = jnp.where(g_idx == g, logits[r0:r0 + EXPERTS_PER_GROUP], fine)
    p_fine = softmax0(fine)
    p1 = jnp.max(p_fine, axis=0, keepdims=True)
    i1 = first_argmax(p_fine, p1, EXPERTS_PER_GROUP)
    rows = lax.broadcasted_iota(jnp.int32, p_fine.shape, 0)
    rest = jnp.where(rows == i1, -1.0, p_fine)
    p2 = jnp.max(rest, axis=0, keepdims=True)
    i2 = first_argmax(rest, p2, EXPERTS_PER_GROUP)
    denom = p1 + p2
    out_rows = lax.broadcasted_iota(jnp.int32, eid_ref.shape, 0)
    e1 = g_idx * EXPERTS_PER_GROUP + i1
    e2 = g_idx * EXPERTS_PER_GROUP + i2
    eid_ref[...] = jnp.where(out_rows == 0, e1, jnp.where(out_rows == 1, e2, 0))
    gate_ref[...] = jnp.where(out_rows == 0, g_prob * p1 / denom,
                              jnp.where(out_rows == 1, g_prob * p2 / denom, 0.0))


def _route(h, gain, w_hi, w_lo, bias):
    t, d = h.shape
    tm = _tile(t, NORM_TM)
    r = w_hi.shape[0]
    out_rows = 8
    return pl.pallas_call(
        _route_kernel,
        grid=(t // tm,),
        in_specs=[pl.BlockSpec((tm, d), lambda i: (i, 0)),
                  pl.BlockSpec((1, d), lambda i: (0, 0)),
                  pl.BlockSpec((r, d), lambda i: (0, 0)),
                  pl.BlockSpec((r, d), lambda i: (0, 0)),
                  pl.BlockSpec((r, 1), lambda i: (0, 0))],
        out_specs=[pl.BlockSpec((out_rows, tm), lambda i: (0, i)),
                   pl.BlockSpec((out_rows, tm), lambda i: (0, i)),
                   pl.BlockSpec((tm, d // 2), lambda i: (i, 0))],
        out_shape=[jax.ShapeDtypeStruct((out_rows, t), jnp.int32),
                   jax.ShapeDtypeStruct((out_rows, t), F32),
                   jax.ShapeDtypeStruct((t, d // 2), jnp.uint32)],
        compiler_params=_params(("arbitrary",), 40),
        name="route",
    )(h, gain.reshape(1, d), w_hi, w_lo, bias)


def _row_copy(src_hbm, row, dst, slot, sem):
    return pltpu.make_async_copy(src_hbm.at[pl.ds(row, 1)], dst.at[pl.ds(slot, 1)], sem)


def _start_rows(src_hbm, idx_ref, idx_base, dst, n_rows, sem):
    def group(g, carry):
        for lane in range(ROWS_PER_ISSUE):
            r = g * ROWS_PER_ISSUE + lane
            _row_copy(src_hbm, idx_ref[0, 0, idx_base + r], dst, r, sem).start(priority=lane % DMA_QUEUES)
        return carry
    lax.fori_loop(0, n_rows // ROWS_PER_ISSUE, group, 0)


def _wait_rows(src_hbm, dst, n_rows, sem):
    def one(r, carry):
        _row_copy(src_hbm, 0, dst, r, sem).wait()
        return carry
    lax.fori_loop(0, n_rows, one, 0)


def _experts_kernel(be_ref, nu_ref, tok_ref, tok_next_ref, u_hbm, wg_ref, wu_ref, wd_ref, o_ref, x_buf, sems):
    i = pl.program_id(0)
    n_used = nu_ref[0]
    tm = x_buf.shape[1]

    @pl.when(i == 0)
    def _first_block():
        _start_rows(u_hbm, tok_ref, 0, x_buf.at[0], tm, sems.at[0])

    @pl.when(i + 1 < n_used)
    def _next_block():
        nxt = (i + 1) & 1
        _start_rows(u_hbm, tok_next_ref, 0, x_buf.at[nxt], tm, sems.at[nxt])

    @pl.when(i < n_used)
    def _run():
        cur = i & 1
        _wait_rows(u_hbm, x_buf.at[cur], tm, sems.at[cur])
        x = jnp.concatenate(_unpack_bf16_pairs(x_buf[cur]), axis=1).astype(BF16)
        a = jnp.dot(x, wg_ref[...], preferred_element_type=F32)
        b = jnp.dot(x, wu_ref[...], preferred_element_type=F32)
        hid = (a / (1.0 + jnp.exp(-a)) * b).astype(BF16)
        o_ref[...] = _pack_bf16_pairs(jnp.dot(hid, wd_ref[...], preferred_element_type=F32))

    @pl.when(i >= n_used)
    def _unused_tail():
        o_ref[...] = jnp.zeros_like(o_ref)


def _experts(u, row_tok, block_e, n_used, wg, wu, wd, layer):
    n_blocks, _, tm = row_tok.shape
    half = u.shape[1]
    d, f = wg.shape[1], wg.shape[2]

    def w_map(i, be, nu):
        return (layer * N_EXPERTS + be[jnp.minimum(i, nu[0] - 1)], 0, 0)

    return pl.pallas_call(
        _experts_kernel,
        grid_spec=pltpu.PrefetchScalarGridSpec(
            num_scalar_prefetch=2,
            grid=(n_blocks,),
            in_specs=[pl.BlockSpec((1, 1, tm), lambda i, be, nu: (i, 0, 0), memory_space=pltpu.SMEM),
                      pl.BlockSpec((1, 1, tm), lambda i, be, nu: (jnp.minimum(i + 1, n_blocks - 1), 0, 0),
                                   memory_space=pltpu.SMEM),
                      pl.BlockSpec(memory_space=pl.ANY),
                      pl.BlockSpec((None, d, f), w_map),
                      pl.BlockSpec((None, d, f), w_map),
                      pl.BlockSpec((None, f, d), w_map)],
            out_specs=pl.BlockSpec((tm, half), lambda i, be, nu: (i, 0)),
            scratch_shapes=[pltpu.VMEM((2, tm, half), jnp.uint32), pltpu.SemaphoreType.DMA((2,))]),
        out_shape=jax.ShapeDtypeStruct((n_blocks * tm, half), jnp.uint32),
        compiler_params=_params(("arbitrary",), 48),
        name="moe_experts",
    )(block_e, n_used, row_tok, row_tok, u, wg, wu, wd)


def _combine_kernel(d_ref, d_next_ref, g_ref, h_ref, ys_hbm, o_ref, buf, sems):
    i = pl.program_id(0)
    tc, half = buf.shape[2], buf.shape[3]

    def fetch(idx_ref, slot):
        for k in range(TOP_K):
            _start_rows(ys_hbm, idx_ref, k * tc, buf.at[slot, k], tc, sems.at[slot])

    @pl.when(i == 0)
    def _first_block():
        fetch(d_ref, 0)

    @pl.when(i + 1 < pl.num_programs(0))
    def _next_block():
        fetch(d_next_ref, (i + 1) & 1)

    cur = i & 1
    for k in range(TOP_K):
        _wait_rows(ys_hbm, buf.at[cur, k], tc, sems.at[cur])
    g = g_ref[...]
    y_hi = h_ref[:, :half]
    y_lo = h_ref[:, half:]
    for k in range(TOP_K):
        hi, lo = _unpack_bf16_pairs(buf[cur, k])
        y_hi += g[:, k:k + 1] * hi
        y_lo += g[:, k:k + 1] * lo
    o_ref[:, :half] = y_hi
    o_ref[:, half:] = y_lo


def _combine(h, ys, dest, gates):
    t, d = h.shape
    n_tb, _, two_tc = dest.shape
    tc = two_tc // TOP_K
    return pl.pallas_call(
        _combine_kernel,
        grid=(n_tb,),
        in_specs=[pl.BlockSpec((1, 1, two_tc), lambda i: (i, 0, 0), memory_space=pltpu.SMEM),
                  pl.BlockSpec((1, 1, two_tc), lambda i: (jnp.minimum(i + 1, n_tb - 1), 0, 0),
                               memory_space=pltpu.SMEM),
                  pl.BlockSpec((tc, TOP_K), lambda i: (i, 0)),
                  pl.BlockSpec((tc, d), lambda i: (i, 0)),
                  pl.BlockSpec(memory_space=pl.ANY)],
        out_specs=pl.BlockSpec((tc, d), lambda i: (i, 0)),
        out_shape=jax.ShapeDtypeStruct((t, d), F32),
        scratch_shapes=[pltpu.VMEM((2, TOP_K, tc, d // 2), jnp.uint32), pltpu.SemaphoreType.DMA((2,))],
        compiler_params=_params(("arbitrary",), 40),
        name="moe_combine",
    )(dest, dest, gates, h, ys)


def _dispatch_plan(eid, t, tm, tc):
    flat_e = eid[:TOP_K].reshape(-1)
    onehot = (flat_e[:, None] == jnp.arange(N_EXPERTS, dtype=jnp.int32)[None, :]).astype(jnp.int32)
    csum = jnp.cumsum(onehot, axis=0)
    counts = csum[-1]
    rank = jnp.sum(onehot * csum, axis=1) - 1
    padded = (counts + tm - 1) // tm * tm
    pends = jnp.cumsum(padded)
    pstarts = pends - padded
    dest = jnp.sum(onehot * pstarts[None, :], axis=1) + rank
    n_blocks = -(-(TOP_K * t + N_EXPERTS * (tm - 1)) // tm)
    flat_tok = jnp.tile(jnp.arange(t, dtype=jnp.int32), TOP_K)
    row_tok = jnp.zeros((n_blocks * tm,), jnp.int32).at[dest].set(flat_tok, unique_indices=True)
    starts = jnp.arange(n_blocks, dtype=jnp.int32) * tm
    block_e = jnp.minimum(jnp.sum((starts[:, None] >= pends[None, :]).astype(jnp.int32), axis=1),
                          N_EXPERTS - 1).astype(jnp.int32)
    n_used = (pends[-1] // tm).astype(jnp.int32).reshape(1)
    dest_blocks = dest.reshape(TOP_K, t // tc, tc).transpose(1, 0, 2).reshape(t // tc, 1, TOP_K * tc)
    return row_tok.reshape(n_blocks, 1, tm), block_e, n_used, dest_blocks.astype(jnp.int32)


def _hi_lo_rows(w_t, rows):
    padded = jnp.zeros((rows, w_t.shape[1]), F32).at[:w_t.shape[0]].set(w_t)
    return _split_bf16(padded)


def _token_mixers(h, layer, bias_tabs, rel_bias, norm_attn, w_qkv, w_gates, b_f, q_gain_diff, k_gain_diff, lam,
                  subln_gain, q_gain_fox, k_gain_fox, out_gain_sb, out_gain_fox, w_out, batch, seq):
    wf_hi, wf_lo = _hi_lo_rows(w_gates.T, GATE_ROWS)
    u, fc_t = _attn_norm(h, norm_attn, wf_hi, wf_lo)
    proj = _in_proj(u, w_qkv, layer)

    b_pad = jnp.zeros((GATE_ROWS, 1), F32).at[:H_FOX, 0].set(b_f)
    cum_t = _gate_cumsum(fc_t, b_pad, batch)
    tq = bias_tabs.shape[-1]
    nq = seq // tq
    cum = cum_t.reshape(GATE_ROWS, batch, nq, 1, tq).transpose(1, 0, 2, 3, 4).reshape(batch * GATE_ROWS, nq, 1, tq)

    lam_init = 0.8 - 0.6 * math.exp(-0.3 * layer)
    lamf = lam.astype(F32)
    lam_val = jnp.exp(jnp.sum(lamf[0] * lamf[1])) - jnp.exp(jnp.sum(lamf[2] * lamf[3])) + lam_init

    o_sb = _sb_attention(proj, out_gain_sb, batch, seq)
    o_diff = _diff_attention(proj, lam_val, lam_init, bias_tabs, rel_bias, q_gain_diff.reshape(-1),
                             k_gain_diff.reshape(-1), subln_gain, batch, seq)
    o_fox = _fox_attention(proj, cum, q_gain_fox, k_gain_fox, out_gain_fox, batch, seq)

    return _out_proj([o_sb, o_diff, o_fox], w_out, layer, h)


def _moe(h, layer, norm_ffn, w_coarse, b_coarse, w_fine, b_fine, w_gate, w_up, w_down):
    t, d = h.shape
    assert EXPERTS_PER_GROUP == 8 and N_GROUPS <= ROUTER_FINE_ROW0 and TOP_K == 2
    w_router = jnp.zeros((ROUTER_ROWS, d), F32)
    w_router = w_router.at[:N_GROUPS].set(w_coarse.T)
    w_router = w_router.at[ROUTER_FINE_ROW0:ROUTER_FINE_ROW0 + N_EXPERTS].set(w_fine.T)
    b_router = jnp.zeros((ROUTER_ROWS, 1), F32)
    b_router = b_router.at[:N_GROUPS, 0].set(b_coarse)
    b_router = b_router.at[ROUTER_FINE_ROW0:ROUTER_FINE_ROW0 + N_EXPERTS, 0].set(b_fine)
    wr_hi, wr_lo = _split_bf16(w_router)
    eid, gate, u = _route(h, norm_ffn, wr_hi, wr_lo, b_router)

    tm = MOE_TM
    tc = _tile(t, COMBINE_TC, 8)
    row_tok, block_e, n_used, dest = _dispatch_plan(eid, t, tm, tc)
    ys = _experts(u, row_tok, block_e, n_used, w_gate, w_up, w_down, layer)
    return _combine(h, ys, dest, gate[:TOP_K].T)


def kernel(x, norm_attn, w_in, b_f, q_gain_diff, k_gain_diff, lam, subln_gain, q_gain_fox, k_gain_fox,
           out_gain_sb, out_gain_fox, rel_bias, w_out, norm_ffn, w_coarse, b_coarse, w_fine, b_fine,
           w_gate, w_up, w_down):
    batch, seq, d = x.shape
    h = x.reshape(batch * seq, d)
    bias_tabs = _diff_bias_tables(rel_bias, _tile(seq, ATT_T))
    experts = [w.astype(BF16).reshape((-1,) + w.shape[2:]) for w in (w_gate, w_up, w_down)]
    n_qkv = 3 * D_MODEL
    w_qkv = w_in[:, :, :n_qkv].astype(BF16)
    for layer in range(DEPTH):
        h = _token_mixers(h, layer, bias_tabs, rel_bias, norm_attn[layer], w_qkv, w_in[layer, :, n_qkv:], b_f[layer],
                          q_gain_diff[layer], k_gain_diff[layer], lam[layer], subln_gain[layer],
                          q_gain_fox[layer], k_gain_fox[layer], out_gain_sb[layer], out_gain_fox[layer],
                          w_out, batch, seq)
        h = _moe(h, layer, norm_ffn[layer], w_coarse[layer], b_coarse[layer], w_fine[layer], b_fine[layer],
                 *experts)
    return h.reshape(batch, seq, d)
```

```python
import functools
import math

import jax
import jax.numpy as jnp
from jax import lax
from jax.experimental import pallas as pl
from jax.experimental.pallas import tpu as pltpu

D_MODEL = 4096
BATCH = 2
SEQ = 8192
DEPTH = 2
CHUNK = 64
HEAD_DIM = 128
HALF_DIM = HEAD_DIM // 2
N_HEADS = D_MODEL // HEAD_DIM
H_DIFF = (N_HEADS * 5) // 16
H_FOX = (N_HEADS * 5) // 16
H_SB = N_HEADS - H_DIFF - H_FOX
N_BUCKETS = 32
MAX_DISTANCE = 128
N_GROUPS = 4
EXPERTS_PER_GROUP = 8
N_EXPERTS = N_GROUPS * EXPERTS_PER_GROUP
TOP_K = 2
D_EXPERT = 512
EPS = 1e-6

F32 = jnp.float32
BF16 = jnp.bfloat16
NEG = -1e30
LOG2E = math.log2(math.e)
SB_UNDERFLOW = 104.0
SAFE_EXP2_RANGE = 64.0
EXP2_UNDERFLOW = 150.0
SUBLANES_BF16 = 16
LANES = 128
GATE_ROWS = 16
ROUTER_FINE_ROW0 = 8
ROUTER_ROWS = 48

NORM_TM = 512
PROJ_TM = 1024
PROJ_TN = 1024
ATT_T = 512
KPREP_ROWS = 512
OUT_TM = 1024
OUT_TN = 512
MOE_TM = 256
ROWS_PER_ISSUE = 8
DMA_QUEUES = 2
COMBINE_TC = 256

_NT = (((1,), (1,)), ((), ()))


def _params(semantics, vmem_mib):
    return pltpu.CompilerParams(dimension_semantics=semantics, vmem_limit_bytes=vmem_mib << 20)


def _tile(n, pref, unit=LANES):
    t = min(pref, n) // unit * unit
    while n % t:
        t -= unit
    return t


def _rms(x, gain):
    return x * lax.rsqrt(jnp.mean(x * x, axis=-1, keepdims=True) + EPS) * gain


def _split_bf16(a):
    hi = a.astype(BF16)
    lo = (a - hi.astype(F32)).astype(BF16)
    return hi, lo


_HIGH_HALF = 0xFFFF0000


def _pack_bf16_pairs(y):
    half = y.shape[1] // 2
    hi = lax.bitcast_convert_type(y[:, :half].astype(BF16).astype(F32), jnp.uint32)
    lo = lax.bitcast_convert_type(y[:, half:].astype(BF16).astype(F32), jnp.uint32)
    return (hi & jnp.uint32(_HIGH_HALF)) | (lo >> 16)


def _unpack_bf16_pairs(w):
    return (lax.bitcast_convert_type(w & jnp.uint32(_HIGH_HALF), F32),
            lax.bitcast_convert_type(w << 16, F32))


def _small_proj_t(w_hi, w_lo, y):
    y_hi, y_lo = _split_bf16(y)
    p = lax.dot_general(w_hi, y_hi, _NT, preferred_element_type=F32)
    p += lax.dot_general(w_lo, y_hi, _NT, preferred_element_type=F32)
    p += lax.dot_general(w_hi, y_lo, _NT, preferred_element_type=F32)
    return p, y_hi


def _attn_norm_kernel(x_ref, g_ref, wh_ref, wl_ref, u_ref, p_ref):
    y = _rms(x_ref[...], g_ref[...])
    p, y_hi = _small_proj_t(wh_ref[...], wl_ref[...], y)
    u_ref[...] = y_hi
    p_ref[...] = p


def _attn_norm(h, gain, w_hi, w_lo):
    t, d = h.shape
    tm = _tile(t, NORM_TM)
    r = w_hi.shape[0]
    return pl.pallas_call(
        _attn_norm_kernel,
        grid=(t // tm,),
        in_specs=[pl.BlockSpec((tm, d), lambda i: (i, 0)),
                  pl.BlockSpec((1, d), lambda i: (0, 0)),
                  pl.BlockSpec((r, d), lambda i: (0, 0)),
                  pl.BlockSpec((r, d), lambda i: (0, 0))],
        out_specs=[pl.BlockSpec((tm, d), lambda i: (i, 0)),
                   pl.BlockSpec((r, tm), lambda i: (0, i))],
        out_shape=[jax.ShapeDtypeStruct((t, d), BF16), jax.ShapeDtypeStruct((r, t), F32)],
        compiler_params=_params(("arbitrary",), 40),
        name="attn_norm",
    )(h, gain.reshape(1, d), w_hi, w_lo)


def _in_proj_kernel(u_ref, wt_ref, o_ref):
    acc = lax.dot_general(u_ref[...], wt_ref[...], _NT, preferred_element_type=F32)
    for s in range(o_ref.shape[0]):
        o_ref[s] = acc[:, s * HEAD_DIM:(s + 1) * HEAD_DIM].astype(BF16)


def _in_proj(u, wt, layer, n):
    t, d = u.shape
    tm, tn = _tile(t, PROJ_TM), _tile(n, PROJ_TN)
    return pl.pallas_call(
        _in_proj_kernel,
        grid=(n // tn, t // tm),
        in_specs=[pl.BlockSpec((tm, d), lambda j, i: (i, 0)),
                  pl.BlockSpec((None, tn, d), lambda j, i: (layer, j, 0))],
        out_specs=pl.BlockSpec((tn // HEAD_DIM, tm, HEAD_DIM), lambda j, i: (j, i, 0)),
        out_shape=jax.ShapeDtypeStruct((n // HEAD_DIM, t, HEAD_DIM), BF16),
        compiler_params=_params(("arbitrary", "arbitrary"), 48),
        name="in_proj",
    )(u, wt)


def _gate_cumsum_kernel(fc_ref, b_ref, tri_ref, o_ref):
    x = fc_ref[...] + b_ref[...]
    log_f = jnp.minimum(x, 0.0) - jnp.log(1.0 + jnp.exp(-jnp.abs(x)))
    tri = tri_ref[...]
    carry = jnp.zeros((x.shape[0], 1), F32)
    for c in range(x.shape[1] // LANES):
        seg = log_f[:, c * LANES:(c + 1) * LANES]
        a = seg.astype(BF16)
        rest = seg - a.astype(F32)
        b, c3 = _split_bf16(rest)
        cs = (jnp.dot(a, tri, preferred_element_type=F32) + jnp.dot(b, tri, preferred_element_type=F32)
              + jnp.dot(c3, tri, preferred_element_type=F32)) + carry
        o_ref[:, c * LANES:(c + 1) * LANES] = cs
        carry = cs[:, LANES - 1:LANES]


def _gate_cumsum(fc_t, b_f, batch):
    r, t = fc_t.shape
    s = t // batch
    idx = jnp.arange(LANES)
    tri = (idx[:, None] <= idx[None, :]).astype(BF16)
    return pl.pallas_call(
        _gate_cumsum_kernel,
        grid=(batch,),
        in_specs=[pl.BlockSpec((r, s), lambda b: (0, b)),
                  pl.BlockSpec((r, 1), lambda b: (0, 0)),
                  pl.BlockSpec((LANES, LANES), lambda b: (0, 0))],
        out_specs=pl.BlockSpec((r, s), lambda b: (0, b)),
        out_shape=jax.ShapeDtypeStruct((r, t), F32),
        compiler_params=_params(("arbitrary",), 32),
        name="gate_cumsum",
    )(fc_t, b_f, tri)


def _kv_rows(j, tk):
    return pl.ds(pl.multiple_of(j * tk, tk), tk)


def _with_ones(v):
    return jnp.concatenate([v, jnp.ones_like(v)], axis=1)


def _row_max(e):
    part = e[:, 0:LANES]
    for c in range(1, e.shape[1] // LANES):
        part = jnp.maximum(part, e[:, c * LANES:(c + 1) * LANES])
    return jnp.max(part, axis=-1, keepdims=True)


def _softmax_first(e, vx, m_ref, acc_ref):
    m = _row_max(e)
    m_ref[...] = jnp.broadcast_to(m, m_ref.shape)
    acc_ref[...] = jnp.dot(jnp.exp2(e - m).astype(BF16), vx, preferred_element_type=F32)


def _softmax_fixed(e, vx, m_ref, acc_ref):
    m = m_ref[...]
    p = jnp.concatenate([jnp.exp2(e[:, c * LANES:(c + 1) * LANES] - m)
                         for c in range(e.shape[1] // LANES)], axis=1)
    acc_ref[...] += jnp.dot(p.astype(BF16), vx, preferred_element_type=F32)


def _softmax_fixed_assign(e, vx, m_ref, acc_ref):
    m = m_ref[...]
    p = jnp.concatenate([jnp.exp2(e[:, c * LANES:(c + 1) * LANES] - m)
                         for c in range(e.shape[1] // LANES)], axis=1)
    acc_ref[...] = jnp.dot(p.astype(BF16), vx, preferred_element_type=F32)


def _softmax_online(e, vx, m_ref, acc_ref):
    m_old = m_ref[:, 0:1]
    m_new = jnp.maximum(m_old, _row_max(e))
    alpha = jnp.exp2(m_old - m_new)
    p = jnp.exp2(e - m_new).astype(BF16)
    acc_ref[...] = alpha * acc_ref[...] + jnp.dot(p, vx, preferred_element_type=F32)
    m_ref[...] = jnp.broadcast_to(m_new, m_ref.shape)


def _sweep_left(i, first_step, safe, step_fn, fixed_count=None):
    top = i - first_step
    count = jnp.maximum(top + 1, 0)
    near = count if fixed_count is None else jnp.minimum(count, fixed_count)

    @pl.when(safe)
    def _fixed():
        quads = lax.shift_right_logical(near, 2)

        def quad(step, carry):
            step_fn(top - 3 - 4 * step, 4, _softmax_fixed)
            return carry
        lax.fori_loop(0, quads, quad, 0)
        rest_top = top - 4 * quads

        @pl.when((near & 2) != 0)
        def _pair():
            step_fn(rest_top - 1, 2, _softmax_fixed)

        @pl.when((near & 1) != 0)
        def _single():
            step_fn(rest_top - (near & 2), 1, _softmax_fixed)

    @pl.when(jnp.logical_not(safe))
    def _online():
        def single(step, carry):
            step_fn(i - first_step - step, 1, _softmax_online)
            return carry
        lax.fori_loop(0, count, single, 0)


def _softmax_result(acc_ref):
    acc = acc_ref[...]
    return acc[:, :HEAD_DIM] / acc[:, HEAD_DIM:]


def _head_specs(t_blocks_per_batch, tq, seq, q_slot, k_slot, v_slot):
    nq = t_blocks_per_batch
    return [pl.BlockSpec((1, tq, HEAD_DIM), lambda b, h, i: (q_slot + h, b * nq + i, 0)),
            pl.BlockSpec((1, seq, HEAD_DIM), lambda b, h, i: (k_slot + h, b, 0)),
            pl.BlockSpec((1, seq, HEAD_DIM), lambda b, h, i: (v_slot + h, b, 0))]


def _sb_kernel(q_ref, k_ref, v_ref, tri_ref, g_ref, o_ref, acc_ref, run_ref):
    i = pl.program_id(2)
    tq = q_ref.shape[1]
    q = (q_ref[0].astype(F32) * (1.0 / math.sqrt(HEAD_DIM))).astype(BF16)
    row = lax.broadcasted_iota(jnp.int32, (tq, tq), 0)
    col = lax.broadcasted_iota(jnp.int32, (tq, tq), 1)
    strict = col < row
    acc_ref[...] = jnp.zeros_like(acc_ref)
    run_ref[...] = jnp.zeros_like(run_ref)

    def block(j, masked):
        k = k_ref[0, _kv_rows(j, tq), :]
        v = v_ref[0, _kv_rows(j, tq), :]
        z = lax.dot_general(q, k, _NT, preferred_element_type=F32)
        sp = jnp.maximum(z, 0.0) + jnp.log(1.0 + jnp.exp(-jnp.abs(z)))
        if masked:
            sp = jnp.where(strict, sp, 0.0)
        suffix = jnp.dot(sp.astype(BF16), tri_ref[...], preferred_element_type=F32)
        w = jnp.exp(z - suffix - run_ref[...])
        if masked:
            w = jnp.where(strict, w, 0.0)
        acc_ref[...] += jnp.dot(w.astype(BF16), v, preferred_element_type=F32)
        run_ref[...] += suffix[:, 0:1]

    block(i, True)

    def more(carry):
        step, smallest_run = carry
        return jnp.logical_and(step < i, smallest_run < SB_UNDERFLOW)

    def body(carry):
        step, _ = carry
        block(i - 1 - step, False)
        return step + 1, jnp.min(run_ref[...])

    lax.while_loop(more, body, (0, jnp.min(run_ref[...])))
    o_ref[...] = _rms(acc_ref[...], g_ref[...]).astype(BF16)


def _sb_attention(proj, gain, batch, seq):
    t = proj.shape[1]
    tq = _tile(seq, ATT_T)
    nq = seq // tq
    idx = jnp.arange(tq)
    tri = (idx[:, None] >= idx[None, :]).astype(BF16)
    return pl.pallas_call(
        _sb_kernel,
        grid=(batch, H_SB, nq),
        in_specs=_head_specs(nq, tq, seq, 0, H_SB, 2 * H_SB) + [
            pl.BlockSpec((tq, tq), lambda b, h, i: (0, 0)),
            pl.BlockSpec((1, HEAD_DIM), lambda b, h, i: (0, 0))],
        out_specs=pl.BlockSpec((tq, HEAD_DIM), lambda b, h, i: (b * nq + i, h)),
        out_shape=jax.ShapeDtypeStruct((t, H_SB * HEAD_DIM), BF16),
        scratch_shapes=[pltpu.VMEM((tq, HEAD_DIM), F32), pltpu.VMEM((tq, 1), F32)],
        compiler_params=_params(("arbitrary", "arbitrary", "arbitrary"), 40),
        name="sb_attention",
    )(proj, proj, proj, tri, gain.reshape(1, HEAD_DIM))


def _sub_rms(x, gain):
    lo = lax.broadcasted_iota(jnp.int32, x.shape, 1) < HALF_DIM
    sq = x * x
    ms_lo = jnp.sum(jnp.where(lo, sq, 0.0), axis=-1, keepdims=True) * (1.0 / HALF_DIM)
    ms_hi = jnp.sum(jnp.where(lo, 0.0, sq), axis=-1, keepdims=True) * (1.0 / HALF_DIM)
    inv = jnp.where(lo, lax.rsqrt(ms_lo + EPS), lax.rsqrt(ms_hi + EPS))
    return x * inv * gain


def _diff_kernel(lam_init, lam_ref, safe_ref, q_ref, k_ref, v_ref, bias_ref, gq_ref, gk_ref, go_ref, o_ref,
                 kp_ref, acc_ref, m_ref):
    i = pl.program_id(2)
    tq = q_ref.shape[1]
    seq = k_ref.shape[1]

    @pl.when(i == 0)
    def _prep_keys():
        def prep(c, carry):
            rows = _kv_rows(c, KPREP_ROWS)
            kn = _sub_rms(k_ref[0, rows, :].astype(F32), gk_ref[...])
            lo = lax.broadcasted_iota(jnp.int32, kn.shape, 1) < HALF_DIM
            kp_ref[0, rows, :] = jnp.where(lo, kn, 0.0).astype(BF16)
            kp_ref[1, rows, :] = jnp.where(lo, 0.0, kn).astype(BF16)
            return carry
        lax.fori_loop(0, seq // KPREP_ROWS, prep, 0)

    q = (_sub_rms(q_ref[0].astype(F32), gq_ref[...]) * (LOG2E / math.sqrt(HALF_DIM))).astype(BF16)
    safe = safe_ref[pl.program_id(1)] != 0

    def block(j, n, table, update):
        rows = pl.ds(pl.multiple_of(j * tq, tq), n * tq)
        vx = _with_ones(v_ref[0, rows, :])
        for sub in range(2):
            e = lax.dot_general(q, kp_ref[sub, rows, :], _NT, preferred_element_type=F32)
            if table is not None:
                e = e + bias_ref[0, table]
            update(e, vx, m_ref.at[sub], acc_ref.at[sub])

    @pl.when(safe)
    def _diag_fixed():
        ones = jnp.ones((HEAD_DIM, LANES), BF16)
        for sub in range(2):
            qk = q.astype(F32) * kp_ref[sub, _kv_rows(i, tq), :].astype(F32)
            m_ref[sub] = jnp.dot(qk.astype(BF16), ones, preferred_element_type=F32)
        block(i, 1, 0, _softmax_fixed_assign)

    @pl.when(jnp.logical_not(safe))
    def _diag_online():
        block(i, 1, 0, _softmax_first)

    for flag, update in ((safe, _softmax_fixed), (jnp.logical_not(safe), _softmax_online)):
        @pl.when(jnp.logical_and(i >= 1, flag))
        def _near(update=update):
            block(i - 1, 1, 1, update)
    _sweep_left(i, 2, safe, lambda j, n, update: block(j, n, None, update))
    o = _softmax_result(acc_ref.at[0]) - lam_ref[0] * _softmax_result(acc_ref.at[1])
    o_ref[...] = (_rms(o, go_ref[...]) * (1.0 - lam_init)).astype(BF16)


def _relative_bucket(rel):
    half = N_BUCKETS // 2
    max_exact = half // 2
    n = jnp.abs(rel)
    large = max_exact + (jnp.log(jnp.maximum(n, 1).astype(F32) / max_exact)
                         / math.log(MAX_DISTANCE / max_exact) * (half - max_exact)).astype(jnp.int32)
    large = jnp.minimum(large, half - 1)
    return jnp.where(rel > 0, half, 0) + jnp.where(n < max_exact, n, large)


def _diff_bias_tables(rel_bias, tq):
    assert tq >= MAX_DISTANCE and tq % CHUNK == 0
    t = jnp.arange(tq)[:, None]
    s = jnp.arange(tq)[None, :]
    rel = jnp.stack([s - t - delta * tq for delta in range(2)])
    bucket = _relative_bucket(rel)[None]
    shifted = ((rel_bias - rel_bias[N_BUCKETS // 2 - 1]) * LOG2E).astype(F32)
    tab = jnp.zeros((rel_bias.shape[1], 2, tq, tq), F32)
    for b in range(N_BUCKETS):
        tab = jnp.where(bucket == b, shifted[b][:, None, None, None], tab)
    visible = jnp.stack([(s // CHUNK) <= (t // CHUNK), jnp.ones((tq, tq), bool)])
    return jnp.where(visible[None], tab, NEG)


def _diff_attention(proj, lam_val, lam_init, bias_tabs, rel_bias, gq, gk, go, batch, seq):
    t = proj.shape[1]
    tq = bias_tabs.shape[-1]
    nq = seq // tq
    base = 3 * H_SB
    vec = pl.BlockSpec((1, HEAD_DIM), lambda b, h, i: (0, 0))
    bound = 2.0 * math.sqrt(HALF_DIM) * jnp.max(jnp.abs(gq)) * jnp.max(jnp.abs(gk))
    spread = jnp.max(rel_bias, axis=0) - jnp.min(rel_bias, axis=0)
    safe = ((bound + spread) * LOG2E < SAFE_EXP2_RANGE).astype(jnp.int32)
    return pl.pallas_call(
        functools.partial(_diff_kernel, lam_init),
        grid=(batch, H_DIFF, nq),
        in_specs=[pl.BlockSpec(memory_space=pltpu.SMEM), pl.BlockSpec(memory_space=pltpu.SMEM)]
        + _head_specs(nq, tq, seq, base, base + H_DIFF, base + 2 * H_DIFF)
        + [pl.BlockSpec((1, 2, tq, tq), lambda b, h, i: (h, 0, 0, 0)), vec, vec, vec],
        out_specs=pl.BlockSpec((tq, HEAD_DIM), lambda b, h, i: (b * nq + i, h)),
        out_shape=jax.ShapeDtypeStruct((t, H_DIFF * HEAD_DIM), BF16),
        scratch_shapes=[pltpu.VMEM((2, seq, HEAD_DIM), BF16), pltpu.VMEM((2, tq, 2 * HEAD_DIM), F32),
                        pltpu.VMEM((2, tq, LANES), F32)],
        compiler_params=_params(("arbitrary", "arbitrary", "arbitrary"), 48),
        name="diff_attention",
    )(lam_val.reshape(1), safe, proj, proj, proj, bias_tabs, gq.reshape(1, HEAD_DIM), gk.reshape(1, HEAD_DIM),
      go.reshape(1, HEAD_DIM))


def _fox_kernel(safe_ref, margin_ref, edge_ref, q_ref, k_ref, v_ref, c_ref, gq_ref, gk_ref, go_ref, o_ref,
                kn_ref, acc_ref, m_ref):
    i = pl.program_id(2)
    tq = q_ref.shape[1]
    seq = k_ref.shape[1]
    nq = seq // tq
    edges = (pl.program_id(0) * GATE_ROWS + pl.program_id(1)) * (2 * nq)
    reach = edge_ref[edges + i] + margin_ref[0]
    live_blocks = jnp.int32(0)
    for j in range(nq):
        live_blocks += jnp.logical_and(j < i, edge_ref[edges + nq + j] <= reach).astype(jnp.int32)

    @pl.when(i == 0)
    def _prep_keys():
        def prep(c, carry):
            rows = _kv_rows(c, KPREP_ROWS)
            kn_ref[rows, :] = _rms(k_ref[0, rows, :].astype(F32), gk_ref[...]).astype(BF16)
            return carry
        lax.fori_loop(0, seq // KPREP_ROWS, prep, 0)

    q = (_rms(q_ref[0].astype(F32), gq_ref[...]) * (LOG2E / math.sqrt(HEAD_DIM))).astype(BF16)
    row = lax.broadcasted_iota(jnp.int32, (tq, tq), 0)
    col = lax.broadcasted_iota(jnp.int32, (tq, tq), 1)
    causal = col <= row
    c_first = c_ref[0, i][:, 0:1]

    def block(j, n, masked, update):
        rows = pl.ds(pl.multiple_of(j * tq, tq), n * tq)
        e = lax.dot_general(q, kn_ref[rows, :], _NT, preferred_element_type=F32)
        c_keys = jnp.concatenate([c_ref[0, j + b] for b in range(n)], axis=1)
        e = e + (c_first - c_keys) * LOG2E
        if masked:
            e = jnp.where(causal, e, NEG)
        update(e, _with_ones(v_ref[0, rows, :]), m_ref, acc_ref)

    block(i, 1, True, _softmax_first)
    _sweep_left(i, 1, safe_ref[0] != 0, lambda j, n, update: block(j, n, False, update), live_blocks)
    o_ref[...] = _rms(_softmax_result(acc_ref), go_ref[...]).astype(BF16)


def _fox_attention(proj, cum, gq, gk, go, batch, seq):
    t = proj.shape[1]
    nq, tq = cum.shape[1], cum.shape[3]
    base = 3 * (H_SB + H_DIFF)
    vec = pl.BlockSpec((1, HEAD_DIM), lambda b, h, i: (0, 0))
    bound = 2.0 * math.sqrt(HEAD_DIM) * jnp.max(jnp.abs(gq)) * jnp.max(jnp.abs(gk))
    safe = (bound * LOG2E < SAFE_EXP2_RANGE).astype(jnp.int32).reshape(1)
    margin = (bound + EXP2_UNDERFLOW / LOG2E).astype(F32).reshape(1)
    edges = jnp.concatenate([cum[:, :, 0, 0], cum[:, :, 0, tq - 1]], axis=1).reshape(-1)
    smem = pl.BlockSpec(memory_space=pltpu.SMEM)
    return pl.pallas_call(
        _fox_kernel,
        grid=(batch, H_FOX, nq),
        in_specs=[smem, smem, smem]
        + _head_specs(nq, tq, seq, base, base + H_FOX, base + 2 * H_FOX)
        + [pl.BlockSpec((1, nq, 1, tq), lambda b, h, i: (b * GATE_ROWS + h, 0, 0, 0)), vec, vec, vec],
        out_specs=pl.BlockSpec((tq, HEAD_DIM), lambda b, h, i: (b * nq + i, h)),
        out_shape=jax.ShapeDtypeStruct((t, H_FOX * HEAD_DIM), BF16),
        scratch_shapes=[pltpu.VMEM((seq, HEAD_DIM), BF16), pltpu.VMEM((tq, 2 * HEAD_DIM), F32),
                        pltpu.VMEM((tq, LANES), F32)],
        compiler_params=_params(("arbitrary", "arbitrary", "arbitrary"), 40),
        name="fox_attention",
    )(safe, margin, edges, proj, proj, proj, cum, gq.reshape(1, HEAD_DIM), gk.reshape(1, HEAD_DIM),
      go.reshape(1, HEAD_DIM))


def _out_proj_kernel(a_sb, a_diff, a_fox, w_ref, h_ref, o_ref, wb_ref):
    @pl.when(pl.program_id(1) == 0)
    def _cast_weights():
        wb_ref[...] = w_ref[...].astype(BF16)

    acc = h_ref[...]
    row = 0
    for a in (a_sb, a_diff, a_fox):
        width = a.shape[1]
        acc += jnp.dot(a[...], wb_ref[row:row + width, :], preferred_element_type=F32)
        row += width
    o_ref[...] = acc


def _out_proj(heads, w, layer, h):
    t, d = h.shape
    tm, tn = _tile(t, OUT_TM), _tile(d, OUT_TN)
    return pl.pallas_call(
        _out_proj_kernel,
        grid=(d // tn, t // tm),
        in_specs=[pl.BlockSpec((tm, a.shape[1]), lambda j, i: (i, 0)) for a in heads]
        + [pl.BlockSpec((None, d, tn), lambda j, i: (layer, 0, j)),
           pl.BlockSpec((tm, tn), lambda j, i: (i, j))],
        out_specs=pl.BlockSpec((tm, tn), lambda j, i: (i, j)),
        out_shape=jax.ShapeDtypeStruct((t, d), F32),
        scratch_shapes=[pltpu.VMEM((d, tn), BF16)],
        compiler_params=_params(("arbitrary", "arbitrary"), 52),
        name="out_proj",
    )(*heads, w, h)


def _route_kernel(x_ref, g_ref, wh_ref, wl_ref, b_ref, eid_ref, gate_ref, u_ref):
    y = _rms(x_ref[...], g_ref[...])
    u_ref[...] = _pack_bf16_pairs(y)
    p, _ = _small_proj_t(wh_ref[...], wl_ref[...], y)
    logits = p + b_ref[...]
    tm = logits.shape[1]

    def softmax0(a):
        ea = jnp.exp(a - jnp.max(a, axis=0, keepdims=True))
        return ea / jnp.sum(ea, axis=0, keepdims=True)

    def first_argmax(a, amax, n):
        rows = lax.broadcasted_iota(jnp.int32, a.shape, 0)
        return jnp.min(jnp.where(a == amax, rows, n), axis=0, keepdims=True)

    p_coarse = softmax0(logits[0:N_GROUPS])
    g_prob = jnp.max(p_coarse, axis=0, keepdims=True)
    g_idx = first_argmax(p_coarse, g_prob, N_GROUPS)
    fine = jnp.zeros((EXPERTS_PER_GROUP, tm), F32)
    for g in range(N_GROUPS):
        r0 = ROUTER_FINE_ROW0 + g * EXPERTS_PER_GROUP
        fine = jnp.where(g_idx == g, logits[r0:r0 + EXPERTS_PER_GROUP], fine)
    p_fine = softmax0(fine)
    p1 = jnp.max(p_fine, axis=0, keepdims=True)
    i1 = first_argmax(p_fine, p1, EXPERTS_PER_GROUP)
    rows = lax.broadcasted_iota(jnp.int32, p_fine.shape, 0)
    rest = jnp.where(rows == i1, -1.0, p_fine)
    p2 = jnp.max(rest, axis=0, keepdims=True)
    i2 = first_argmax(rest, p2, EXPERTS_PER_GROUP)
    denom = p1 + p2
    out_rows = lax.broadcasted_iota(jnp.int32, eid_ref.shape, 0)
    e1 = g_idx * EXPERTS_PER_GROUP + i1
    e2 = g_idx * EXPERTS_PER_GROUP + i2
    eid_ref[...] = jnp.where(out_rows == 0, e1, jnp.where(out_rows == 1, e2, 0))
    gate_ref[...] = jnp.where(out_rows == 0, g_prob * p1 / denom,
                              jnp.where(out_rows == 1, g_prob * p2 / denom, 0.0))


def _route(h, gain, w_hi, w_lo, bias):
    t, d = h.shape
    tm = _tile(t, NORM_TM)
    r = w_hi.shape[0]
    out_rows = 8
    return pl.pallas_call(
        _route_kernel,
        grid=(t // tm,),
        in_specs=[pl.BlockSpec((tm, d), lambda i: (i, 0)),
                  pl.BlockSpec((1, d), lambda i: (0, 0)),
                  pl.BlockSpec((r, d), lambda i: (0, 0)),
                  pl.BlockSpec((r, d), lambda i: (0, 0)),
                  pl.BlockSpec((r, 1), lambda i: (0, 0))],
        out_specs=[pl.BlockSpec((out_rows, tm), lambda i: (0, i)),
                   pl.BlockSpec((out_rows, tm), lambda i: (0, i)),
                   pl.BlockSpec((tm, d // 2), lambda i: (i, 0))],
        out_shape=[jax.ShapeDtypeStruct((out_rows, t), jnp.int32),
                   jax.ShapeDtypeStruct((out_rows, t), F32),
                   jax.ShapeDtypeStruct((t, d // 2), jnp.uint32)],
        compiler_params=_params(("arbitrary",), 40),
        name="route",
    )(h, gain.reshape(1, d), w_hi, w_lo, bias)


def _row_copy(src_hbm, row, dst, slot, sem):
    return pltpu.make_async_copy(src_hbm.at[pl.ds(row, 1)], dst.at[pl.ds(slot, 1)], sem)


def _start_rows(src_hbm, idx_ref, idx_base, dst, n_rows, sem):
    def group(g, carry):
        for lane in range(ROWS_PER_ISSUE):
            r = g * ROWS_PER_ISSUE + lane
            _row_copy(src_hbm, idx_ref[0, 0, idx_base + r], dst, r, sem).start(priority=lane % DMA_QUEUES)
        return carry
    lax.fori_loop(0, n_rows // ROWS_PER_ISSUE, group, 0)


def _wait_rows(src_hbm, dst, n_rows, sem):
    def one(r, carry):
        _row_copy(src_hbm, 0, dst, r, sem).wait()
        return carry
    lax.fori_loop(0, n_rows, one, 0)


def _experts_kernel(be_ref, nu_ref, tok_ref, tok_next_ref, u_hbm, wg_ref, wu_ref, wd_ref, o_ref, x_buf, sems):
    i = pl.program_id(0)
    n_used = nu_ref[0]
    tm = x_buf.shape[1]

    @pl.when(i == 0)
    def _first_block():
        _start_rows(u_hbm, tok_ref, 0, x_buf.at[0], tm, sems.at[0])

    @pl.when(i + 1 < n_used)
    def _next_block():
        nxt = (i + 1) & 1
        _start_rows(u_hbm, tok_next_ref, 0, x_buf.at[nxt], tm, sems.at[nxt])

    @pl.when(i < n_used)
    def _run():
        cur = i & 1
        _wait_rows(u_hbm, x_buf.at[cur], tm, sems.at[cur])
        x = jnp.concatenate(_unpack_bf16_pairs(x_buf[cur]), axis=1).astype(BF16)
        a = jnp.dot(x, wg_ref[...], preferred_element_type=F32)
        b = jnp.dot(x, wu_ref[...], preferred_element_type=F32)
        hid = (a / (1.0 + jnp.exp(-a)) * b).astype(BF16)
        o_ref[...] = _pack_bf16_pairs(jnp.dot(hid, wd_ref[...], preferred_element_type=F32))

    @pl.when(i >= n_used)
    def _unused_tail():
        o_ref[...] = jnp.zeros_like(o_ref)


def _experts(u, row_tok, block_e, n_used, wg, wu, wd, layer):
    n_blocks, _, tm = row_tok.shape
    half = u.shape[1]
    d, f = wg.shape[1], wg.shape[2]

    def w_map(i, be, nu):
        return (layer * N_EXPERTS + be[jnp.minimum(i, nu[0] - 1)], 0, 0)

    return pl.pallas_call(
        _experts_kernel,
        grid_spec=pltpu.PrefetchScalarGridSpec(
            num_scalar_prefetch=2,
            grid=(n_blocks,),
            in_specs=[pl.BlockSpec((1, 1, tm), lambda i, be, nu: (i, 0, 0), memory_space=pltpu.SMEM),
                      pl.BlockSpec((1, 1, tm), lambda i, be, nu: (jnp.minimum(i + 1, n_blocks - 1), 0, 0),
                                   memory_space=pltpu.SMEM),
                      pl.BlockSpec(memory_space=pl.ANY),
                      pl.BlockSpec((None, d, f), w_map),
                      pl.BlockSpec((None, d, f), w_map),
                      pl.BlockSpec((None, f, d), w_map)],
            out_specs=pl.BlockSpec((tm, half), lambda i, be, nu: (i, 0)),
            scratch_shapes=[pltpu.VMEM((2, tm, half), jnp.uint32), pltpu.SemaphoreType.DMA((2,))]),
        out_shape=jax.ShapeDtypeStruct((n_blocks * tm, half), jnp.uint32),
        compiler_params=_params(("arbitrary",), 48),
        name="moe_experts",
    )(block_e, n_used, row_tok, row_tok, u, wg, wu, wd)


def _combine_kernel(d_ref, d_next_ref, g_ref, h_ref, ys_hbm, o_ref, buf, sems):
    i = pl.program_id(0)
    tc, half = buf.shape[2], buf.shape[3]

    def fetch(idx_ref, slot):
        for k in range(TOP_K):
            _start_rows(ys_hbm, idx_ref, k * tc, buf.at[slot, k], tc, sems.at[slot])

    @pl.when(i == 0)
    def _first_block():
        fetch(d_ref, 0)

    @pl.when(i + 1 < pl.num_programs(0))
    def _next_block():
        fetch(d_next_ref, (i + 1) & 1)

    cur = i & 1
    for k in range(TOP_K):
        _wait_rows(ys_hbm, buf.at[cur, k], tc, sems.at[cur])
    g = g_ref[...]
    y_hi = h_ref[:, :half]
    y_lo = h_ref[:, half:]
    for k in range(TOP_K):
        hi, lo = _unpack_bf16_pairs(buf[cur, k])
        y_hi += g[:, k:k + 1] * hi
        y_lo += g[:, k:k + 1] * lo
    o_ref[:, :half] = y_hi
    o_ref[:, half:] = y_lo


def _combine(h, ys, dest, gates):
    t, d = h.shape
    n_tb, _, two_tc = dest.shape
    tc = two_tc // TOP_K
    return pl.pallas_call(
        _combine_kernel,
        grid=(n_tb,),
        in_specs=[pl.BlockSpec((1, 1, two_tc), lambda i: (i, 0, 0), memory_space=pltpu.SMEM),
                  pl.BlockSpec((1, 1, two_tc), lambda i: (jnp.minimum(i + 1, n_tb - 1), 0, 0),
                               memory_space=pltpu.SMEM),
                  pl.BlockSpec((tc, TOP_K), lambda i: (i, 0)),
                  pl.BlockSpec((tc, d), lambda i: (i, 0)),
                  pl.BlockSpec(memory_space=pl.ANY)],
        out_specs=pl.BlockSpec((tc, d), lambda i: (i, 0)),
        out_shape=jax.ShapeDtypeStruct((t, d), F32),
        scratch_shapes=[pltpu.VMEM((2, TOP_K, tc, d // 2), jnp.uint32), pltpu.SemaphoreType.DMA((2,))],
        compiler_params=_params(("arbitrary",), 40),
        name="moe_combine",
    )(dest, dest, gates, h, ys)


def _dispatch_plan(eid, t, tm, tc):
    flat_e = eid[:TOP_K].reshape(-1)
    onehot = (flat_e[:, None] == jnp.arange(N_EXPERTS, dtype=jnp.int32)[None, :]).astype(jnp.int32)
    csum = jnp.cumsum(onehot, axis=0)
    counts = csum[-1]
    rank = jnp.sum(onehot * csum, axis=1) - 1
    padded = (counts + tm - 1) // tm * tm
    pends = jnp.cumsum(padded)
    pstarts = pends - padded
    dest = jnp.sum(onehot * pstarts[None, :], axis=1) + rank
    n_blocks = -(-(TOP_K * t + N_EXPERTS * (tm - 1)) // tm)
    flat_tok = jnp.tile(jnp.arange(t, dtype=jnp.int32), TOP_K)
    row_tok = jnp.zeros((n_blocks * tm,), jnp.int32).at[dest].set(flat_tok, unique_indices=True)
    starts = jnp.arange(n_blocks, dtype=jnp.int32) * tm
    block_e = jnp.minimum(jnp.sum((starts[:, None] >= pends[None, :]).astype(jnp.int32), axis=1),
                          N_EXPERTS - 1).astype(jnp.int32)
    n_used = (pends[-1] // tm).astype(jnp.int32).reshape(1)
    dest_blocks = dest.reshape(TOP_K, t // tc, tc).transpose(1, 0, 2).reshape(t // tc, 1, TOP_K * tc)
    return row_tok.reshape(n_blocks, 1, tm), block_e, n_used, dest_blocks.astype(jnp.int32)


def _hi_lo_rows(w_t, rows):
    padded = jnp.zeros((rows, w_t.shape[1]), F32).at[:w_t.shape[0]].set(w_t)
    return _split_bf16(padded)


def _token_mixers(h, layer, bias_tabs, rel_bias, norm_attn, w_in_t, w_gates, b_f, q_gain_diff, k_gain_diff, lam,
                  subln_gain, q_gain_fox, k_gain_fox, out_gain_sb, out_gain_fox, w_out, batch, seq):
    wf_hi, wf_lo = _hi_lo_rows(w_gates.T, GATE_ROWS)
    u, fc_t = _attn_norm(h, norm_attn, wf_hi, wf_lo)
    proj = _in_proj(u, w_in_t, layer, 3 * D_MODEL)

    b_pad = jnp.zeros((GATE_ROWS, 1), F32).at[:H_FOX, 0].set(b_f)
    cum_t = _gate_cumsum(fc_t, b_pad, batch)
    tq = bias_tabs.shape[-1]
    nq = seq // tq
    cum = cum_t.reshape(GATE_ROWS, batch, nq, 1, tq).transpose(1, 0, 2, 3, 4).reshape(batch * GATE_ROWS, nq, 1, tq)

    lam_init = 0.8 - 0.6 * math.exp(-0.3 * layer)
    lamf = lam.astype(F32)
    lam_val = jnp.exp(jnp.sum(lamf[0] * lamf[1])) - jnp.exp(jnp.sum(lamf[2] * lamf[3])) + lam_init

    o_sb = _sb_attention(proj, out_gain_sb, batch, seq)
    o_diff = _diff_attention(proj, lam_val, lam_init, bias_tabs, rel_bias, q_gain_diff.reshape(-1),
                             k_gain_diff.reshape(-1), subln_gain, batch, seq)
    o_fox = _fox_attention(proj, cum, q_gain_fox, k_gain_fox, out_gain_fox, batch, seq)

    return _out_proj([o_sb, o_diff, o_fox], w_out, layer, h)


def _moe(h, layer, norm_ffn, w_coarse, b_coarse, w_fine, b_fine, w_gate, w_up, w_down):
    t, d = h.shape
    assert EXPERTS_PER_GROUP == 8 and N_GROUPS <= ROUTER_FINE_ROW0 and TOP_K == 2
    w_router = jnp.zeros((ROUTER_ROWS, d), F32)
    w_router = w_router.at[:N_GROUPS].set(w_coarse.T)
    w_router = w_router.at[ROUTER_FINE_ROW0:ROUTER_FINE_ROW0 + N_EXPERTS].set(w_fine.T)
    b_router = jnp.zeros((ROUTER_ROWS, 1), F32)
    b_router = b_router.at[:N_GROUPS, 0].set(b_coarse)
    b_router = b_router.at[ROUTER_FINE_ROW0:ROUTER_FINE_ROW0 + N_EXPERTS, 0].set(b_fine)
    wr_hi, wr_lo = _split_bf16(w_router)
    eid, gate, u = _route(h, norm_ffn, wr_hi, wr_lo, b_router)

    tm = MOE_TM
    tc = _tile(t, COMBINE_TC, 8)
    row_tok, block_e, n_used, dest = _dispatch_plan(eid, t, tm, tc)
    ys = _experts(u, row_tok, block_e, n_used, w_gate, w_up, w_down, layer)
    return _combine(h, ys, dest, gate[:TOP_K].T)


def kernel(x, norm_attn, w_in, b_f, q_gain_diff, k_gain_diff, lam, subln_gain, q_gain_fox, k_gain_fox,
           out_gain_sb, out_gain_fox, rel_bias, w_out, norm_ffn, w_coarse, b_coarse, w_fine, b_fine,
           w_gate, w_up, w_down):
    batch, seq, d = x.shape
    h = x.reshape(batch * seq, d)
    bias_tabs = _diff_bias_tables(rel_bias, _tile(seq, ATT_T))
    experts = [w.astype(BF16).reshape((-1,) + w.shape[2:]) for w in (w_gate, w_up, w_down)]
    n_qkv = 3 * D_MODEL
    w_in_t = jnp.swapaxes(w_in, 1, 2).astype(BF16)
    for layer in range(DEPTH):
        h = _token_mixers(h, layer, bias_tabs, rel_bias, norm_attn[layer], w_in_t, w_in[layer, :, n_qkv:], b_f[layer],
                          q_gain_diff[layer], k_gain_diff[layer], lam[layer], subln_gain[layer],
                          q_gain_fox[layer], k_gain_fox[layer], out_gain_sb[layer], out_gain_fox[layer],
                          w_out, batch, seq)
        h = _moe(h, layer, norm_ffn[layer], w_coarse[layer], b_coarse[layer], w_fine[layer], b_fine[layer],
                 *experts)
    return h.reshape(batch, seq, d)
```
